```python
import jax, jax.numpy as jnp
from jax import lax
import numpy as np

D_MODEL = 2048
BATCH = 8
SEQ = 2048
DEPTH = 1
DEC_BATCH = 4
DEC_SEQ = 4096
PAST_LEN = 128

MIX_W = D_MODEL
RWKV_W = MIX_W // 2
RWKV_HEAD = 64
RWKV_HEADS = RWKV_W // RWKV_HEAD
DECAY_LORA = 64
ICLR_LORA = 64
GATE_LORA = 128
SSM_W = MIX_W - RWKV_W
SSM_HEAD = 64
SSM_HEADS = SSM_W // SSM_HEAD
SSM_GROUPS = 2
SSM_STATE = 128
SSM_CONV = 7
SSM_CHUNK = 128
SSM_CONV_CH = SSM_W + 2 * SSM_GROUPS * SSM_STATE
FFN_CONV = 3
D_FF = 5632
N_DIR = 2
N_MOD = 6
NORM_EPS = 1e-6
GN_EPS = 64e-5
GATED_EPS = 1e-5

WD0 = 3 * RWKV_W
AD0 = WD0 + N_DIR * DECAY_LORA
GD0 = AD0 + N_DIR * ICLR_LORA
RWKV_COLS = GD0 + GATE_LORA
SSM_COLS = SSM_W + SSM_CONV_CH + N_DIR * SSM_HEADS
P_IN = RWKV_COLS + SSM_COLS

kernel_name = "hybrid_bidir_rwkv7_mamba2_adaln_encoder"


def rmsnorm(x, w, eps=NORM_EPS):
    xf = x.astype(jnp.float32)
    y = xf * lax.rsqrt(jnp.mean(jnp.square(xf), axis=-1, keepdims=True) + eps)
    return (y * w.astype(jnp.float32)).astype(x.dtype)


def dwconv_centred(x, w, b):
    k_w = w.shape[0]
    t = x.shape[1]
    left = (k_w - 1) // 2
    xp = jnp.pad(x, ((0, 0), (left, k_w - 1 - left), (0, 0)))
    y = b
    for i in range(k_w):
        y = y + xp[:, i:i + t] * w[i]
    return y


def token_shift_centred(p, mu):
    prev = jnp.pad(p[:, :-1], ((0, 0), (1, 0), (0, 0)))
    nxt = jnp.pad(p[:, 1:], ((0, 0), (0, 1), (0, 0)))
    return p + mu[0] * (prev - p) + mu[1] * (nxt - p)


def rwkv7_step(state, inp):
    r_t, w_t, k_t, v_t, a_t, b_t = inp
    sa = jnp.einsum("bdhij,bdhj->bdhi", state, a_t)
    state = state * w_t[..., None, :] + sa[..., :, None] * b_t[..., None, :] + v_t[..., :, None] * k_t[..., None, :]
    y = jnp.einsum("bdhij,bdhj->bdhi", state, r_t)
    return state, y


def rwkv7_mixer(p, mu, w0, w2, a0, a2, g2, k_k, k_a, r_k, lnx_w, lnx_b):
    f32 = jnp.float32
    b, t, _ = p.shape
    h_n, n = RWKV_HEADS, RWKV_HEAD
    xs = token_shift_centred(p, mu)
    r = xs[..., :RWKV_W].astype(f32)
    k = xs[..., RWKV_W:2 * RWKV_W].astype(f32)
    v = xs[..., 2 * RWKV_W:3 * RWKV_W].astype(f32)
    wd = xs[..., WD0:AD0].reshape(b, t, N_DIR, DECAY_LORA)
    ad = xs[..., AD0:GD0].reshape(b, t, N_DIR, ICLR_LORA)
    gd = xs[..., GD0:]
    w_log = (w0 + jnp.einsum("btdr,drc->btdc", jnp.tanh(wd), w2)).astype(f32)
    decay = jnp.exp(-jnp.exp(-jax.nn.softplus(-w_log) - 0.5))
    a = jax.nn.sigmoid((a0 + jnp.einsum("btdr,drc->btdc", ad, a2)).astype(f32))
    g = (jax.nn.sigmoid(gd) @ g2).astype(f32)
    kk = (k * k_k.astype(f32)).reshape(b, t, h_n, n)
    kk = (kk / jnp.maximum(jnp.linalg.norm(kk, axis=-1, keepdims=True), 1e-12)).reshape(b, t, RWKV_W)
    k_dir = k[:, :, None] * (1.0 + (a - 1.0) * k_a.astype(f32))

    def shared(z):
        return jnp.stack([z, z], axis=2)

    def per_dir(z):
        z = jnp.stack([z[:, :, 0], jnp.flip(z[:, :, 1], axis=1)], axis=2)
        return z.reshape(b, t, N_DIR, h_n, n).transpose(1, 0, 2, 3, 4)

    seqs = (per_dir(shared(r)), per_dir(decay), per_dir(k_dir), per_dir(shared(v)),
            per_dir(shared(-kk)), per_dir(shared(kk) * a))
    s0 = jnp.zeros((b, N_DIR, h_n, n, n), f32)
    _, y = lax.scan(rwkv7_step, s0, seqs)
    y = y.transpose(1, 0, 2, 3, 4)
    y = y[:, :, 0] + jnp.flip(y[:, :, 1], axis=1)
    mean = jnp.mean(y, axis=-1, keepdims=True)
    var = jnp.mean(jnp.square(y - mean), axis=-1, keepdims=True)
    y = ((y - mean) * lax.rsqrt(var + GN_EPS)).reshape(b, t, RWKV_W) * lnx_w.astype(f32) + lnx_b.astype(f32)
    bonus = jnp.sum(r.reshape(b, t, h_n, n) * k_dir.sum(axis=2).reshape(b, t, h_n, n) * r_k.astype(f32),
                    axis=-1, keepdims=True) * v.reshape(b, t, h_n, n)
    return ((y + bonus.reshape(b, t, RWKV_W)) * g).astype(p.dtype)


def segsum(z):
    t = z.shape[-1]
    zc = jnp.cumsum(z, axis=-1)
    diff = zc[..., :, None] - zc[..., None, :]
    return jnp.where(jnp.tril(jnp.ones((t, t), dtype=bool)), diff, -jnp.inf)


def ssd_chunked(xs, dt, a_h, bm, cm):
    b, t, h, hp = xs.shape
    g, n = bm.shape[2], bm.shape[3]
    j = h // g
    q = SSM_CHUNK
    c = t // q
    xd = (xs * dt[..., None]).reshape(b, c, q, g, j, hp)
    bc = bm.reshape(b, c, q, g, n)
    cc = cm.reshape(b, c, q, g, n)
    adt = (dt * a_h).reshape(b, c, q, g, j).transpose(0, 1, 3, 4, 2)
    acs = jnp.cumsum(adt, axis=-1)
    lmat = jnp.exp(segsum(adt))
    cb = jnp.einsum("bclgn,bcsgn->bcgls", cc, bc)
    y_diag = jnp.einsum("bcgls,bcgjls,bcsgjp->bclgjp", cb, lmat, xd)
    decay_in = jnp.exp(acs[..., -1:] - acs)
    states = jnp.einsum("bcsgn,bcgjs,bcsgjp->bcgjpn", bc, decay_in, xd)
    chunk_a = jnp.pad(acs[..., -1], ((0, 0), (1, 0), (0, 0), (0, 0)))
    decay_chunk = jnp.exp(segsum(chunk_a.transpose(0, 2, 3, 1)))
    states = jnp.concatenate([jnp.zeros_like(states[:, :1]), states], axis=1)
    states_in = jnp.einsum("bgjzc,bcgjpn->bzgjpn", decay_chunk, states)[:, :-1]
    y_off = jnp.einsum("bclgn,bcgjpn,bcgjl->bclgjp", cc, states_in, jnp.exp(acs))
    return (y_diag + y_off).reshape(b, t, h, hp)


def mamba2_mixer(p, conv_w, conv_b, dt_bias, a_log, d_skip, norm_w):
    f32 = jnp.float32
    b, t, _ = p.shape
    gn = SSM_GROUPS * SSM_STATE
    z = p[..., :SSM_W]
    xbc = jax.nn.silu(dwconv_centred(p[..., SSM_W:SSM_W + SSM_CONV_CH], conv_w, conv_b))
    dt_raw = p[..., SSM_W + SSM_CONV_CH:].reshape(b, t, N_DIR, SSM_HEADS)
    xs = xbc[..., :SSM_W].reshape(b, t, SSM_HEADS, SSM_HEAD).astype(f32)
    bm = xbc[..., SSM_W:SSM_W + gn].reshape(b, t, SSM_GROUPS, SSM_STATE).astype(f32)
    cm = xbc[..., SSM_W + gn:].reshape(b, t, SSM_GROUPS, SSM_STATE).astype(f32)
    dt = jax.nn.softplus(dt_raw.astype(f32) + dt_bias.astype(f32))
    a_h = -jnp.exp(a_log.astype(f32))
    y_f = ssd_chunked(xs, dt[:, :, 0], a_h[0], bm, cm)
    fl = lambda u: jnp.flip(u, axis=1)
    y_b = fl(ssd_chunked(fl(xs), fl(dt[:, :, 1]), a_h[1], fl(bm), fl(cm)))
    y = y_f + y_b + xs * d_skip.astype(f32)[:, None]
    y = y.reshape(b, t, SSM_W) * jax.nn.silu(z.astype(f32))
    y = y * lax.rsqrt(jnp.mean(jnp.square(y), axis=-1, keepdims=True) + GATED_EPS) * norm_w.astype(f32)
    return y.astype(p.dtype)


def conv_glu_ffn(h, w_up, conv_w, conv_b, w_down):
    u = dwconv_centred(h @ w_up, conv_w, conv_b)
    return (jax.nn.silu(u[..., :D_FF]) * u[..., D_FF:]) @ w_down


def encoder(x, c, p):
    sc = jax.nn.silu(c)
    for l in range(DEPTH):
        mod = sc @ p["w_ada"][l] + p["b_ada"][l]
        sh1, sc1, g1, sh2, sc2, g2 = jnp.split(mod[:, None, :], N_MOD, axis=-1)
        h = rmsnorm(x, p["norm1_w"][l]) * (1.0 + sc1) + sh1
        proj = h @ p["w_in"][l]
        y_rw = rwkv7_mixer(proj[..., :RWKV_COLS], p["rwkv_mu"][l], p["rwkv_w0"][l], p["rwkv_w2"][l],
                           p["rwkv_a0"][l], p["rwkv_a2"][l], p["rwkv_g2"][l], p["rwkv_k_k"][l],
                           p["rwkv_k_a"][l], p["rwkv_r_k"][l], p["rwkv_lnx_w"][l], p["rwkv_lnx_b"][l])
        y_ssm = mamba2_mixer(proj[..., RWKV_COLS:], p["ssm_conv_w"][l], p["ssm_conv_b"][l],
                             p["ssm_dt_bias"][l], p["ssm_a_log"][l], p["ssm_d"][l], p["ssm_norm_w"][l])
        x = x + g1 * (jnp.concatenate([y_rw, y_ssm], axis=-1) @ p["w_out"][l])
        h = rmsnorm(x, p["norm2_w"][l]) * (1.0 + sc2) + sh2
        x = x + g2 * conv_glu_ffn(h, p["ffn_w_up"][l], p["ffn_conv_w"][l], p["ffn_conv_b"][l], p["ffn_w_down"][l])
    modf = sc @ p["w_ada_final"] + p["b_ada_final"]
    shf, scf = jnp.split(modf[:, None, :], 2, axis=-1)
    return rmsnorm(x, p["final_norm_w"]) * (1.0 + scf) + shf


def setup_inputs(seed: int = 0) -> dict:
    key = jax.random.key(seed)
    ks = iter(jax.random.split(key, 48))
    f32 = jnp.float32
    nrm = lambda shape, s: jax.random.normal(next(ks), shape, f32) * s
    uni = lambda shape, lo, hi: jax.random.uniform(next(ks), shape, f32, lo, hi)
    gain = lambda shape: 1.0 + nrm(shape, 0.02)
    dt0 = jnp.exp(uni((DEPTH, N_DIR, SSM_HEADS), float(np.log(1e-3)), float(np.log(1e-1))))
    return {
        "x_prompt": nrm((BATCH, SEQ, D_MODEL), 1.0),
        "x_sample": nrm((DEC_BATCH, DEC_SEQ, D_MODEL), 1.0),
        "c_prompt": nrm((BATCH, D_MODEL), 1.0),
        "c_sample": nrm((DEC_BATCH, D_MODEL), 1.0),
        "norm1_w": gain((DEPTH, D_MODEL)),
        "w_in": nrm((DEPTH, D_MODEL, P_IN), D_MODEL ** -0.5),
        "rwkv_mu": uni((DEPTH, 2, RWKV_COLS), 0.0, 0.5),
        "rwkv_w0": uni((DEPTH, N_DIR, RWKV_W), -6.0, 1.0),
        "rwkv_w2": nrm((DEPTH, N_DIR, DECAY_LORA, RWKV_W), 0.5 * DECAY_LORA ** -0.5),
        "rwkv_a0": nrm((DEPTH, N_DIR, RWKV_W), 0.5),
        "rwkv_a2": nrm((DEPTH, N_DIR, ICLR_LORA, RWKV_W), 0.5 * ICLR_LORA ** -0.5),
        "rwkv_g2": nrm((DEPTH, GATE_LORA, RWKV_W), GATE_LORA ** -0.5),
        "rwkv_k_k": 0.85 + nrm((DEPTH, RWKV_W), 0.02),
        "rwkv_k_a": gain((DEPTH, RWKV_W)),
        "rwkv_r_k": nrm((DEPTH, RWKV_HEADS, RWKV_HEAD), 0.1),
        "rwkv_lnx_w": gain((DEPTH, RWKV_W)),
        "rwkv_lnx_b": nrm((DEPTH, RWKV_W), 0.01),
        "ssm_conv_w": nrm((DEPTH, SSM_CONV, SSM_CONV_CH), SSM_CONV ** -0.5),
        "ssm_conv_b": nrm((DEPTH, SSM_CONV_CH), 0.01),
        "ssm_dt_bias": dt0 + jnp.log(-jnp.expm1(-dt0)),
        "ssm_a_log": jnp.log(uni((DEPTH, N_DIR, SSM_HEADS), 1.0, 16.0)),
        "ssm_d": gain((DEPTH, SSM_HEADS)),
        "ssm_norm_w": gain((DEPTH, SSM_W)),
        "w_out": nrm((DEPTH, MIX_W, D_MODEL), MIX_W ** -0.5),
        "norm2_w": gain((DEPTH, D_MODEL)),
        "ffn_w_up": nrm((DEPTH, D_MODEL, 2 * D_FF), D_MODEL ** -0.5),
        "ffn_conv_w": nrm((DEPTH, FFN_CONV, 2 * D_FF), FFN_CONV ** -0.5),
        "ffn_conv_b": nrm((DEPTH, 2 * D_FF), 0.01),
        "ffn_w_down": nrm((DEPTH, D_FF, D_MODEL), D_FF ** -0.5),
        "w_ada": nrm((DEPTH, D_MODEL, N_MOD * D_MODEL), 0.5 * D_MODEL ** -0.5),
        "b_ada": nrm((DEPTH, N_MOD * D_MODEL), 0.01),
        "final_norm_w": gain((D_MODEL,)),
        "w_ada_final": nrm((D_MODEL, 2 * D_MODEL), 0.5 * D_MODEL ** -0.5),
        "b_ada_final": nrm((2 * D_MODEL,), 0.01),
    }


def reference(x_prompt, x_sample, c_prompt, c_sample, norm1_w, w_in, rwkv_mu, rwkv_w0, rwkv_w2, rwkv_a0,
              rwkv_a2, rwkv_g2, rwkv_k_k, rwkv_k_a, rwkv_r_k, rwkv_lnx_w, rwkv_lnx_b, ssm_conv_w, ssm_conv_b,
              ssm_dt_bias, ssm_a_log, ssm_d, ssm_norm_w, w_out, norm2_w, ffn_w_up, ffn_conv_w, ffn_conv_b,
              ffn_w_down, w_ada, b_ada, final_norm_w, w_ada_final, b_ada_final):
    p = {
        "norm1_w": norm1_w, "w_in": w_in, "rwkv_mu": rwkv_mu, "rwkv_w0": rwkv_w0, "rwkv_w2": rwkv_w2,
        "rwkv_a0": rwkv_a0, "rwkv_a2": rwkv_a2, "rwkv_g2": rwkv_g2, "rwkv_k_k": rwkv_k_k,
        "rwkv_k_a": rwkv_k_a, "rwkv_r_k": rwkv_r_k, "rwkv_lnx_w": rwkv_lnx_w, "rwkv_lnx_b": rwkv_lnx_b,
        "ssm_conv_w": ssm_conv_w, "ssm_conv_b": ssm_conv_b, "ssm_dt_bias": ssm_dt_bias,
        "ssm_a_log": ssm_a_log, "ssm_d": ssm_d, "ssm_norm_w": ssm_norm_w, "w_out": w_out,
        "norm2_w": norm2_w, "ffn_w_up": ffn_w_up, "ffn_conv_w": ffn_conv_w, "ffn_conv_b": ffn_conv_b,
        "ffn_w_down": ffn_w_down, "w_ada": w_ada, "b_ada": b_ada, "final_norm_w": final_norm_w,
        "w_ada_final": w_ada_final, "b_ada_final": b_ada_final,
    }
    y_prompt = encoder(x_prompt, c_prompt, p)
    y_sample = encoder(x_sample, c_sample, p)
    return (y_prompt, y_sample)
```

```python
import functools
import math

import jax
import jax.numpy as jnp
from jax import lax
from jax.experimental import pallas as pl
from jax.experimental.pallas import tpu as pltpu

F32 = jnp.float32
BF16 = jnp.bfloat16

LANES = 128
SUBLANES_BF16 = 16
MXU_DIM = 256
VMEM_LIMIT_BYTES = 56 * 1024 * 1024

NORM_EPS = 1e-6
GN_EPS = 64e-5
GATED_EPS = 1e-5

HEAD = 64
PAIR = 2 * HEAD
RWKV_CHUNK = 64
SSD_CHUNK = 128
HALO_ROWS = 8
NEG_BIG = -1e30


def _cparams(sem):
    return pltpu.CompilerParams(dimension_semantics=sem, vmem_limit_bytes=VMEM_LIMIT_BYTES)


def _dot(a, b):
    return jnp.dot(a, b, preferred_element_type=F32)


def _dot_nt(a, b):
    return lax.dot_general(a, b, (((1,), (1,)), ((), ())), preferred_element_type=F32)


def _sigmoid(x):
    return 1.0 / (1.0 + jnp.exp(-x))


def _silu(x):
    return x * _sigmoid(x)


def _split3(x):
    hi = x.astype(BF16)
    r1 = x - hi.astype(F32)
    mid = r1.astype(BF16)
    lo = (r1 - mid.astype(F32)).astype(BF16)
    return hi, mid, lo


def _cumsum_rows(x, tri3, n):
    hi, mid, lo = _split3(x)
    stacked = jnp.concatenate([hi, mid, lo], axis=0)
    out = _dot(tri3, stacked)
    return out[:n], out[n:n + 1]


def _make_tri3(n, reverse):
    rows = lax.broadcasted_iota(jnp.int32, (n + 8, 3 * n), 0)
    cols = lax.broadcasted_iota(jnp.int32, (n + 8, 3 * n), 1) & (n - 1)
    tri = (cols >= rows) if reverse else (cols <= rows)
    return jnp.where(tri | (rows >= n), 1.0, 0.0).astype(BF16)


def _segsum_heads(x, bd):
    hi = x.astype(BF16)
    lo = (x - hi.astype(F32)).astype(BF16)
    outs = []
    for i in range(x.shape[1] // MXU_DIM):
        sl = slice(i * MXU_DIM, (i + 1) * MXU_DIM)
        outs.append(_dot(hi[:, sl], bd) + _dot(lo[:, sl], bd))
    return jnp.concatenate(outs, axis=1)


def _rms_mod(x, w, scale, shift):
    ms = jnp.mean(x * x, axis=-1, keepdims=True)
    return x * lax.rsqrt(ms + NORM_EPS) * w * (1.0 + scale) + shift


def _ada_kernel(c_ref, w_ref, b_ref, o_ref):
    c = c_ref[...]
    o_ref[...] = _dot(_silu(c), w_ref[...]) + b_ref[...]


def _ada(c_pad, w, b):
    rows, d = c_pad.shape
    n = w.shape[1]
    tn = 1024
    return pl.pallas_call(
        _ada_kernel,
        grid=(n // tn,),
        in_specs=[
            pl.BlockSpec((rows, d), lambda j: (0, 0)),
            pl.BlockSpec((d, tn), lambda j: (0, j)),
            pl.BlockSpec((1, tn), lambda j: (0, j)),
        ],
        out_specs=pl.BlockSpec((rows, tn), lambda j: (0, j)),
        out_shape=jax.ShapeDtypeStruct((rows, n), F32),
        compiler_params=_cparams(("arbitrary",)),
        name="ada_mod",
    )(c_pad, w, b.reshape(1, n))


def _inproj_kernel(n_rw, x_ref, mod_ref, nw_ref, w_ref, orw_ref, ossm_ref):
    x = x_ref[0]
    h = _rms_mod(x, nw_ref[...], mod_ref[0, 1:2, :], mod_ref[0, 0:1, :]).astype(BF16)
    orw_ref[0] = _dot(h, w_ref[:, :n_rw])
    ossm_ref[0] = _dot(h, w_ref[:, n_rw:])


def _in_proj(x, mod, norm_w, w_cat, n_rw, tm):
    b, t, d = x.shape
    n_ssm = w_cat.shape[1] - n_rw
    return pl.pallas_call(
        functools.partial(_inproj_kernel, n_rw),
        grid=(b, t // tm),
        in_specs=[
            pl.BlockSpec((1, tm, d), lambda i, j: (i, j, 0)),
            pl.BlockSpec((1, 6, d), lambda i, j: (i, 0, 0)),
            pl.BlockSpec((1, d), lambda i, j: (0, 0)),
            pl.BlockSpec(memory_space=pltpu.VMEM),
        ],
        out_specs=[
            pl.BlockSpec((1, tm, n_rw), lambda i, j: (i, j, 0)),
            pl.BlockSpec((1, tm, n_ssm), lambda i, j: (i, j, 0)),
        ],
        out_shape=[
            jax.ShapeDtypeStruct((b, t, n_rw), F32),
            jax.ShapeDtypeStruct((b, t, n_ssm), F32),
        ],
        compiler_params=_cparams(("arbitrary", "arbitrary")),
        name="in_proj",
    )(x, mod, norm_w.reshape(1, d), w_cat)


def _stack2(x, m0):
    return jnp.concatenate([jnp.where(m0, x, 0.0), jnp.where(m0, 0.0, x)], axis=0)


def _rwkv_kernel(reverse, n_chunks, width, *refs):
    if reverse:
        (p_ref, pp_ref, pn_ref, mu_ref, w0_ref, w2_ref, a0_ref, a2_ref, kk_ref, ka_ref, bd_ref,
         y_ref, st_ref) = refs
    else:
        (p_ref, pp_ref, pn_ref, mu_ref, w0_ref, w2_ref, a0_ref, a2_ref, kk_ref, ka_ref, bd_ref,
         a0o_ref, a2o_ref, g2_ref, rk_ref, lw_ref, lb_ref, yb_ref, y_ref, st_ref) = refs
    n = RWKV_CHUNK
    step = pl.program_id(1)
    chunk = (n_chunks - 1 - step) if reverse else step

    @pl.when(step == 0)
    def _():
        st_ref[...] = jnp.zeros_like(st_ref)

    p = p_ref[0]
    rows = lax.broadcasted_iota(jnp.int32, (n, 1), 0)
    prev_row = jnp.where(chunk > 0, pp_ref[0, HALO_ROWS - 1:HALO_ROWS, :], 0.0)
    next_row = jnp.where(chunk < n_chunks - 1, pn_ref[0, 0:1, :], 0.0)
    p_prev = jnp.where(rows == 0, prev_row, pltpu.roll(p, 1, 0))
    p_next = jnp.where(rows == n - 1, next_row, pltpu.roll(p, n - 1, 0))
    xs = p + mu_ref[0:1, :] * (p_prev - p) + mu_ref[1:2, :] * (p_next - p)

    r = xs[:, 0:width]
    k = xs[:, width:2 * width]
    v = xs[:, 2 * width:3 * width]
    o = 3 * width
    wd = xs[:, o:o + LANES]
    ad = xs[:, o + LANES:o + 2 * LANES]
    gd = xs[:, o + 2 * LANES:o + 3 * LANES]

    w_log = w0_ref[...] + _dot(jnp.tanh(wd).astype(BF16), w2_ref[...])
    lw = -math.exp(-0.5) * _sigmoid(w_log)
    ad_b = ad.astype(BF16)
    a = _sigmoid(a0_ref[...] + _dot(ad_b, a2_ref[...]))

    bd = bd_ref[...]
    kk = k * kk_ref[...]
    kk = kk / jnp.maximum(jnp.sqrt(_segsum_heads(kk * kk, bd)), 1e-12)
    k_a = ka_ref[...]
    k_d = k * (1.0 + (a - 1.0) * k_a)
    alpha = -kk
    beta = kk * a

    tri3 = _make_tri3(n, reverse)
    c, ctot = _cumsum_rows(lw, tri3, n)
    e_nc = jnp.exp(-c)
    a_t = alpha * jnp.exp(c - lw)
    r_t = r * jnp.exp(c)
    b_t = beta * e_nc
    k_t = k_d * e_nc
    e_end = jnp.exp(ctot - c)
    b_h = beta * e_end
    k_h = k_d * e_end
    p_end = jnp.exp(ctot)

    lane = lax.broadcasted_iota(jnp.int32, (1, PAIR), 1)
    m0 = lane < HEAD
    t_idx = lax.broadcasted_iota(jnp.int32, (n, 2 * n), 0)
    col = lax.broadcasted_iota(jnp.int32, (n, 2 * n), 1)
    s_idx = col & (n - 1)
    col_h0 = col < n
    if reverse:
        strict, incl = s_idx > t_idx, s_idx >= t_idx
    else:
        strict, incl = s_idx < t_idx, s_idx <= t_idx
    eye2 = jnp.where(s_idx == t_idx, 1.0, 0.0)
    br = lax.broadcasted_iota(jnp.int32, (PAIR, PAIR), 0) < HEAD
    bc = lax.broadcasted_iota(jnp.int32, (PAIR, PAIR), 1) < HEAD
    blk = br == bc

    def blockdiag(x):
        return jnp.concatenate(
            [jnp.where(col_h0, x, 0.0), jnp.where(col_h0, 0.0, x)], axis=0).astype(BF16)

    ys = []
    for j in range(width // PAIR):
        sl = slice(j * PAIR, (j + 1) * PAIR)
        vj = v[:, sl]
        lhs = jnp.concatenate([a_t[:, sl], r_t[:, sl]], axis=0).astype(BF16)
        rhs = jnp.concatenate([_stack2(b_t[:, sl], m0), _stack2(k_t[:, sl], m0)], axis=0).astype(BF16)
        g1 = _dot_nt(lhs, rhs)
        st = st_ref[j]
        g2 = _dot_nt(lhs, st.astype(BF16))
        a_ab = jnp.where(strict, g1[:n, :2 * n], 0.0)
        a_ak = jnp.where(strict, g1[:n, 2 * n:], 0.0)
        a_rb = jnp.where(incl, g1[n:, :2 * n], 0.0)
        a_rk = jnp.where(incl, g1[n:, 2 * n:], 0.0)
        tmat = eye2 + a_ab
        pw = a_ab
        for _ in range(int(math.log2(n)) - 1):
            pw = _dot(pw.astype(BF16), blockdiag(pw))
            tmat = tmat + _dot(tmat.astype(BF16), blockdiag(pw))
        v2 = _stack2(vj, m0).astype(BF16)
        w_rhs = g2[:n] + _dot(a_ak.astype(BF16), v2)
        u = _dot(tmat.astype(BF16), _stack2(w_rhs, m0).astype(BF16))
        u2 = _stack2(u, m0).astype(BF16)
        y = g2[n:] + _dot(jnp.concatenate([a_rb, a_rk], axis=1).astype(BF16),
                          jnp.concatenate([u2, v2], axis=0))
        ys.append(y)
        uv_t = jnp.concatenate([u, vj], axis=0).T.astype(BF16)
        bk = jnp.concatenate([b_h[:, sl], k_h[:, sl]], axis=0).astype(BF16)
        st_ref[j] = st * p_end[:, sl] + jnp.where(blk, _dot(uv_t, bk), 0.0)
    y_all = jnp.concatenate(ys, axis=1)

    if reverse:
        y_ref[0] = y_all
    else:
        ysum = y_all + yb_ref[0]
        inv = 1.0 / HEAD
        mean = _segsum_heads(ysum, bd) * inv
        dlt = ysum - mean
        var = _segsum_heads(dlt * dlt, bd) * inv
        yn = dlt * lax.rsqrt(var + GN_EPS) * lw_ref[...] + lb_ref[...]
        a_o = _sigmoid(a0o_ref[...] + _dot(ad_b, a2o_ref[...]))
        ksum = k_d + k * (1.0 + (a_o - 1.0) * k_a)
        bonus = _segsum_heads(r * ksum * rk_ref[...], bd) * v
        g = _dot(_sigmoid(gd).astype(BF16), g2_ref[...])
        y_ref[0] = ((yn + bonus) * g).astype(y_ref.dtype)


def _rwkv_scan(reverse, p_rw, consts, yb=None):
    b, t, ncol = p_rw.shape
    width = consts["width"]
    n = RWKV_CHUNK
    n_chunks = t // n
    hb = n // HALO_ROWS
    last_hb = t // HALO_ROWS - 1

    def cidx(j):
        return (n_chunks - 1 - j) if reverse else j

    def row(x):
        return pl.BlockSpec((1, x.shape[1]), lambda i, j: (0, 0))

    def full(x):
        return pl.BlockSpec(x.shape, lambda i, j: (0,) * x.ndim)

    d = 1 if reverse else 0
    args = [p_rw, p_rw, p_rw, consts["mu"], consts["w0"][d], consts["w2"][d], consts["a0"][d],
            consts["a2"][d], consts["k_k"], consts["k_a"], consts["bd"]]
    specs = [
        pl.BlockSpec((1, n, ncol), lambda i, j: (i, cidx(j), 0)),
        pl.BlockSpec((1, HALO_ROWS, ncol), lambda i, j: (i, jnp.maximum(cidx(j) * hb - 1, 0), 0)),
        pl.BlockSpec((1, HALO_ROWS, ncol),
                     lambda i, j: (i, jnp.minimum((cidx(j) + 1) * hb, last_hb), 0)),
        full(consts["mu"]), row(consts["w0"][d]), full(consts["w2"][d]), row(consts["a0"][d]),
        full(consts["a2"][d]), row(consts["k_k"]), row(consts["k_a"]), full(consts["bd"]),
    ]
    if not reverse:
        args += [consts["a0"][1], consts["a2"][1], consts["g2"], consts["r_k"], consts["lnx_w"],
                 consts["lnx_b"], yb]
        specs += [row(consts["a0"][1]), full(consts["a2"][1]), full(consts["g2"]), row(consts["r_k"]),
                  row(consts["lnx_w"]), row(consts["lnx_b"]),
                  pl.BlockSpec((1, n, width), lambda i, j: (i, cidx(j), 0))]
    out_dtype = F32 if reverse else BF16
    return pl.pallas_call(
        functools.partial(_rwkv_kernel, reverse, n_chunks, width),
        grid=(b, n_chunks),
        in_specs=specs,
        out_specs=pl.BlockSpec((1, n, width), lambda i, j: (i, cidx(j), 0)),
        out_shape=jax.ShapeDtypeStruct((b, t, width), out_dtype),
        scratch_shapes=[pltpu.VMEM((width // PAIR, PAIR, PAIR), F32)],
        compiler_params=_cparams(("arbitrary", "arbitrary")),
        name="rwkv_bwd" if reverse else "rwkv_fwd",
    )(*args)


def _ssd_kernel(reverse, n_chunks, width, n_state, conv_k, *refs):
    if reverse:
        (p_ref, pp_ref, pn_ref, cw_ref, cb_ref, dtb_ref, alog_ref, ed_ref,
         y_ref, st_ref, ext_ref) = refs
    else:
        (p_ref, pp_ref, pn_ref, cw_ref, cb_ref, dtb_ref, alog_ref, ed_ref, dsk_ref, nw_ref, yb_ref,
         y_ref, st_ref, ext_ref) = refs
    q = SSD_CHUNK
    step = pl.program_id(1)
    chunk = (n_chunks - 1 - step) if reverse else step
    n_groups = 2
    conv_ch = width + 2 * n_groups * n_state
    left = (conv_k - 1) // 2

    @pl.when(step == 0)
    def _():
        st_ref[...] = jnp.zeros_like(st_ref)

    c0, c1 = width, width + conv_ch
    ext_ref[0:HALO_ROWS, :] = jnp.where(chunk > 0, pp_ref[0, :, c0:c1], 0.0)
    ext_ref[HALO_ROWS:HALO_ROWS + q, :] = p_ref[0, :, c0:c1]
    ext_ref[HALO_ROWS + q:, :] = jnp.where(chunk < n_chunks - 1, pn_ref[0, :, c0:c1], 0.0)
    acc = cb_ref[...] + ext_ref[pl.ds(HALO_ROWS - left, q), :] * cw_ref[0:1, :]
    for i in range(1, conv_k):
        acc = acc + ext_ref[pl.ds(HALO_ROWS - left + i, q), :] * cw_ref[i:i + 1, :]
    xbc = _silu(acc)
    xs = xbc[:, :width]

    z_dt =p_ref[0, :, c1:c1 + LANES] + dtb_ref[...]
    dt = jnp.maximum(z_dt, 0.0) + jnp.log(1.0 + jnp.exp(-jnp.abs(z_dt)))
    adt = dt * (-jnp.exp(alog_ref[...]))
    tri3 = _make_tri3(q, reverse)
    acs, atot = _cumsum_rows(adt, tri3, q)

    ed = ed_ref[...]

    def expand(x):
        hi, mid, lo = _split3(x)
        return _dot(hi, ed) + _dot(mid, ed) + _dot(lo, ed)

    dt_x = expand(dt)
    acs_x = expand(jnp.concatenate([acs, jnp.broadcast_to(atot, (HALO_ROWS, LANES))], axis=0))
    atot_x = acs_x[q:q + 1]
    acs_x = acs_x[:q]
    xd = xs * dt_x
    e_acs = jnp.exp(acs_x)
    xdec = xd * jnp.exp(atot_x - acs_x)
    p_end = jnp.exp(atot_x)

    acs_t = acs.T
    t_idx = lax.broadcasted_iota(jnp.int32, (q, q), 0)
    s_idx = lax.broadcasted_iota(jnp.int32, (q, q), 1)
    incl = (s_idx >= t_idx) if reverse else (s_idx <= t_idx)
    lane = lax.broadcasted_iota(jnp.int32, (1, PAIR), 1)
    m0 = lane < HEAD
    d = 1 if reverse else 0
    heads = width // HEAD
    pairs_per_group = heads // n_groups // 2

    ys = []
    for g in range(n_groups):
        bm = xbc[:, width + g * n_state:width + (g + 1) * n_state]
        cm = xbc[:, width + (n_groups + g) * n_state:width + (n_groups + g + 1) * n_state]
        bm_b = bm.astype(BF16)
        cm_b = cm.astype(BF16)
        cb = _dot_nt(cm_b, bm_b)
        bm_t = bm.T.astype(BF16)
        for jj in range(pairs_per_group):
            j = g * pairs_per_group + jj
            sl = slice(j * PAIR, (j + 1) * PAIR)
            ms = []
            for hh in range(2):
                li = d * heads + 2 * j + hh
                diff = acs[:, li:li + 1] - acs_t[li:li + 1, :]
                lmat = jnp.exp(jnp.where(incl, diff, NEG_BIG))
                ms.append((cb * lmat).astype(BF16))
            lhs = jnp.concatenate(ms, axis=1)
            y_diag = _dot(lhs, _stack2(xd[:, sl], m0).astype(BF16))
            st = st_ref[j]
            y_off = _dot(cm_b, st.astype(BF16)) * e_acs[:, sl]
            ys.append(y_diag + y_off)
            st_ref[j] = st * p_end[:, sl] + _dot(bm_t, xdec[:, sl].astype(BF16))
    y_all = jnp.concatenate(ys, axis=1)

    gate = _silu(p_ref[0, :, 0:width])
    if reverse:
        y_ref[0] = y_all * gate
    else:
        y = (y_all + xs * dsk_ref[...]) * gate + yb_ref[0]
        ms = jnp.mean(y * y, axis=-1, keepdims=True)
        y_ref[0] = (y * lax.rsqrt(ms + GATED_EPS) * nw_ref[...]).astype(y_ref.dtype)


def _ssd_scan(reverse, p_ssm, consts, yb=None):
    b, t, ncol = p_ssm.shape
    width = consts["width"]
    n_state = consts["n_state"]
    conv_k = consts["conv_w"].shape[0]
    conv_ch = consts["conv_w"].shape[1]
    q = SSD_CHUNK
    n_chunks = t // q
    hb = q // HALO_ROWS
    last_hb = t // HALO_ROWS - 1

    def cidx(j):
        return (n_chunks - 1 - j) if reverse else j

    def row(x):
        return pl.BlockSpec((1, x.shape[1]), lambda i, j: (0, 0))

    def full(x):
        return pl.BlockSpec(x.shape, lambda i, j: (0,) * x.ndim)

    expand = consts["expand"][1 if reverse else 0]
    args = [p_ssm, p_ssm, p_ssm, consts["conv_w"], consts["conv_b"], consts["dt_bias"],
            consts["a_log"], expand]
    specs = [
        pl.BlockSpec((1, q, ncol), lambda i, j: (i, cidx(j), 0)),
        pl.BlockSpec((1, HALO_ROWS, ncol), lambda i, j: (i, jnp.maximum(cidx(j) * hb - 1, 0), 0)),
        pl.BlockSpec((1, HALO_ROWS, ncol),
                     lambda i, j: (i, jnp.minimum((cidx(j) + 1) * hb, last_hb), 0)),
        full(consts["conv_w"]), row(consts["conv_b"]), row(consts["dt_bias"]), row(consts["a_log"]),
        full(expand),
    ]
    if not reverse:
        args += [consts["d_skip"], consts["norm_w"], yb]
        specs += [row(consts["d_skip"]), row(consts["norm_w"]),
                  pl.BlockSpec((1, q, width), lambda i, j: (i, cidx(j), 0))]
    out_dtype = F32 if reverse else BF16
    return pl.pallas_call(
        functools.partial(_ssd_kernel, reverse, n_chunks, width, n_state, conv_k),
        grid=(b, n_chunks),
        in_specs=specs,
        out_specs=pl.BlockSpec((1, q, width), lambda i, j: (i, cidx(j), 0)),
        out_shape=jax.ShapeDtypeStruct((b, t, width), out_dtype),
        scratch_shapes=[pltpu.VMEM((width // PAIR, n_state, PAIR), F32),
                        pltpu.VMEM((q + 2 * HALO_ROWS, conv_ch), F32)],
        compiler_params=_cparams(("arbitrary", "arbitrary")),
        name="ssd_bwd" if reverse else "ssd_fwd",
    )(*args)


def _outproj_kernel(half, x_ref, yr_ref, ys_ref, mod_ref, nw_ref, w_ref, x1_ref, h2_ref):
    attn = _dot(yr_ref[0], w_ref[:half, :]) + _dot(ys_ref[0], w_ref[half:, :])
    x1 = x_ref[0] + mod_ref[0, 2:3, :] * attn
    x1_ref[0] = x1
    h2_ref[0] = _rms_mod(x1, nw_ref[...], mod_ref[0, 4:5, :], mod_ref[0, 3:4, :]).astype(BF16)


def _out_proj(x, y_rw, y_ssm, mod, norm_w, w_out, tm):
    b, t, d = x.shape
    half = y_rw.shape[2]
    return pl.pallas_call(
        functools.partial(_outproj_kernel, half),
        grid=(b, t // tm),
        in_specs=[
            pl.BlockSpec((1, tm, d), lambda i, j: (i, j, 0)),
            pl.BlockSpec((1, tm, half), lambda i, j: (i, j, 0)),
            pl.BlockSpec((1, tm, half), lambda i, j: (i, j, 0)),
            pl.BlockSpec((1, 6, d), lambda i, j: (i, 0, 0)),
            pl.BlockSpec((1, d), lambda i, j: (0, 0)),
            pl.BlockSpec(memory_space=pltpu.VMEM),
        ],
        out_specs=[
            pl.BlockSpec((1, tm, d), lambda i, j: (i, j, 0)),
            pl.BlockSpec((1, tm, d), lambda i, j: (i, j, 0)),
        ],
        out_shape=[
            jax.ShapeDtypeStruct((b, t, d), F32),
            jax.ShapeDtypeStruct((b, t, d), BF16),
        ],
        compiler_params=_cparams(("arbitrary", "arbitrary")),
        name="out_proj",
    )(x, y_rw, y_ssm, mod, norm_w.reshape(1, d), w_out)


def _ffn_kernel(tm, n_tiles, n_ff, h_ref, hp_ref, hn_ref, x1_ref, mod_ref, modf_ref, fw_ref,
                wv_ref, wg_ref, cwv_ref, cwg_ref, cbv_ref, cbg_ref, wd_ref,
                o_ref, lhs_ref, acc_ref, uv_ref, ug_ref):
    i = pl.program_id(1)
    f = pl.program_id(2)
    hal = SUBLANES_BF16

    @pl.when(f == 0)
    def _():
        lhs_ref[0:hal, :] = jnp.where(i > 0, hp_ref[0], jnp.zeros_like(hp_ref[0]))
        lhs_ref[hal:hal + tm, :] = h_ref[0]
        lhs_ref[hal + tm:, :] = jnp.where(i < n_tiles - 1, hn_ref[0], jnp.zeros_like(hn_ref[0]))
        acc_ref[...] = jnp.zeros_like(acc_ref)

    lhs = lhs_ref[...]
    uv_ref[...] = _dot(lhs, wv_ref[...])
    ug_ref[...] = _dot(lhs, wg_ref[...])

    def conv(u_ref, cw_ref, cb_ref):
        out = cb_ref[...] + u_ref[pl.ds(hal - 1, tm), :] * cw_ref[0:1, :]
        out = out + u_ref[pl.ds(hal, tm), :] * cw_ref[1:2, :]
        return out + u_ref[pl.ds(hal + 1, tm), :] * cw_ref[2:3, :]

    act = _silu(conv(uv_ref, cwv_ref, cbv_ref)) * conv(ug_ref, cwg_ref, cbg_ref)
    acc_ref[...] += _dot(act.astype(BF16), wd_ref[...])

    @pl.when(f == n_ff - 1)
    def _():
        x2 = x1_ref[0] + mod_ref[0, 5:6, :] * acc_ref[...]
        o_ref[0] = _rms_mod(x2, fw_ref[...], modf_ref[0, 1:2, :], modf_ref[0, 0:1, :])


def _ffn(h2, x1, mod, modf, final_w, w_up, conv_w, conv_b, w_down, tm, ck):
    b, t, d = x1.shape
    d_ff = w_down.shape[0]
    n_ff = d_ff // ck
    n_tiles = t // tm
    hal = SUBLANES_BF16
    hb = tm // hal
    last_hb = t // hal - 1
    cb2 = conv_b.reshape(1, 2 * d_ff)
    return pl.pallas_call(
        functools.partial(_ffn_kernel, tm, n_tiles, n_ff),
        grid=(b, n_tiles, n_ff),
        in_specs=[
            pl.BlockSpec((1, tm, d), lambda i, j, f: (i, j, 0)),
            pl.BlockSpec((1, hal, d), lambda i, j, f: (i, jnp.maximum(j * hb - 1, 0), 0)),
            pl.BlockSpec((1, hal, d), lambda i, j, f: (i, jnp.minimum((j + 1) * hb, last_hb), 0)),
            pl.BlockSpec((1, tm, d), lambda i, j, f: (i, j, 0)),
            pl.BlockSpec((1, 6, d), lambda i, j, f: (i, 0, 0)),
            pl.BlockSpec((1, 2, d), lambda i, j, f: (i, 0, 0)),
            pl.BlockSpec((1, d), lambda i, j, f: (0, 0)),
            pl.BlockSpec((d, ck), lambda i, j, f: (0, f)),
            pl.BlockSpec((d, ck), lambda i, j, f: (0, f + n_ff)),
            pl.BlockSpec((conv_w.shape[0], ck), lambda i, j, f: (0, f)),
            pl.BlockSpec((conv_w.shape[0], ck), lambda i, j, f: (0, f + n_ff)),
            pl.BlockSpec((1, ck), lambda i, j, f: (0, f)),
            pl.BlockSpec((1, ck), lambda i, j, f: (0, f + n_ff)),
            pl.BlockSpec((ck, d), lambda i, j, f: (f, 0)),
        ],
        out_specs=pl.BlockSpec((1, tm, d), lambda i, j, f: (i, j, 0)),
        out_shape=jax.ShapeDtypeStruct((b, t, d), F32),
        scratch_shapes=[
            pltpu.VMEM((tm + 2 * hal, d), BF16),
            pltpu.VMEM((tm, d), F32),
            pltpu.VMEM((tm + 2 * hal, ck), F32),
            pltpu.VMEM((tm + 2 * hal, ck), F32),
        ],
        compiler_params=_cparams(("arbitrary", "arbitrary", "arbitrary")),
        name="conv_glu_ffn",
    )(h2, h2, h2, x1, mod, modf, final_w.reshape(1, d), w_up, w_up, conv_w, conv_w, cb2, cb2, w_down)


def _pad_rows(x, rows):
    return jnp.concatenate([x, jnp.zeros((rows - x.shape[0],) + x.shape[1:], x.dtype)], axis=0)


def _pad_lanes(x, lanes):
    return jnp.concatenate([x, jnp.zeros(x.shape[:-1] + (lanes - x.shape[-1],), x.dtype)], axis=-1)


def _encoder(x, mod, modf, wts):
    rw, ssm = wts["rwkv"], wts["ssm"]
    p_rw, p_ssm = _in_proj(x, mod, wts["norm1_w"], wts["w_in"], rw["ncol"], tm=256)
    yb = _rwkv_scan(True, p_rw, rw)
    y_rw = _rwkv_scan(False, p_rw, rw, yb)
    sb = _ssd_scan(True, p_ssm, ssm)
    y_ssm = _ssd_scan(False, p_ssm, ssm, sb)
    x1, h2 = _out_proj(x, y_rw, y_ssm, mod, wts["norm2_w"], wts["w_out"], tm=512)
    return _ffn(h2, x1, mod, modf, wts["final_norm_w"], wts["w_up"], wts["ffn_conv_w"],
                wts["ffn_conv_b"], wts["w_down"], tm=512, ck=512)


def kernel(x_prompt, x_sample, c_prompt, c_sample, norm1_w, w_in, rwkv_mu, rwkv_w0, rwkv_w2, rwkv_a0, rwkv_a2, rwkv_g2, rwkv_k_k, rwkv_k_a, rwkv_r_k, rwkv_lnx_w, rwkv_lnx_b, ssm_conv_w, ssm_conv_b, ssm_dt_bias, ssm_a_log, ssm_d, ssm_norm_w, w_out, norm2_w, ffn_w_up, ffn_conv_w, ffn_conv_b, ffn_w_down, w_ada, b_ada, final_norm_w, w_ada_final, b_ada_final):
    assert w_in.shape[0] == 1, "single-layer trunk"
    d = x_prompt.shape[-1]
    width = rwkv_w0.shape[-1]
    heads = width // HEAD
    lora = rwkv_w2.shape[2]
    n_rw = rwkv_mu.shape[-1]
    n_state = (ssm_conv_w.shape[-1] - width) // 4
    assert 2 * lora == LANES and rwkv_g2.shape[1] == LANES and 2 * heads <= LANES

    n_ssm = w_in.shape[2] - n_rw
    n_ssm_pad = -(-n_ssm // LANES) * LANES
    w_cat = _pad_lanes(w_in[0], n_rw + n_ssm_pad).astype(BF16)
    zl = jnp.zeros((lora, width), F32)
    eye_h = jnp.repeat(jnp.eye(heads, dtype=F32), HEAD, axis=1)
    seg = jnp.arange(MXU_DIM) // HEAD
    rwkv = {
        "width": width, "ncol": n_rw,
        "mu": rwkv_mu[0],
        "w0": [rwkv_w0[0, i:i + 1] for i in range(2)],
        "a0": [rwkv_a0[0, i:i + 1] for i in range(2)],
        "w2": [jnp.concatenate([rwkv_w2[0, 0], zl], 0).astype(BF16),
               jnp.concatenate([zl, rwkv_w2[0, 1]], 0).astype(BF16)],
        "a2": [jnp.concatenate([rwkv_a2[0, 0], zl], 0).astype(BF16),
               jnp.concatenate([zl, rwkv_a2[0, 1]], 0).astype(BF16)],
        "g2": rwkv_g2[0].astype(BF16),
        "k_k": rwkv_k_k, "k_a": rwkv_k_a, "r_k": rwkv_r_k[0].reshape(1, width),
        "lnx_w": rwkv_lnx_w, "lnx_b": rwkv_lnx_b,
        "bd": (seg[:, None] == seg[None, :]).astype(BF16),
    }
    ssm = {
        "width": width, "n_state": n_state,
        "conv_w": ssm_conv_w[0], "conv_b": ssm_conv_b,
        "dt_bias": _pad_lanes(ssm_dt_bias[0].reshape(1, 2 * heads), LANES),
        "a_log": _pad_lanes(ssm_a_log[0].reshape(1, 2 * heads), LANES),
        "d_skip": jnp.repeat(ssm_d[0], HEAD).reshape(1, width),
        "norm_w": ssm_norm_w,
        "expand": [_pad_rows(jnp.concatenate([eye_h * (1 - i), eye_h * i], 0), LANES).astype(BF16)
                   for i in range(2)],
    }
    wts = {
        "norm1_w": norm1_w[0], "norm2_w": norm2_w[0], "final_norm_w": final_norm_w,
        "w_in": w_cat, "w_out": w_out[0].astype(BF16),
        "w_up": ffn_w_up[0].astype(BF16), "w_down": ffn_w_down[0].astype(BF16),
        "ffn_conv_w": ffn_conv_w[0], "ffn_conv_b": ffn_conv_b[0],
        "rwkv": rwkv, "ssm": ssm,
    }

    nb_p, nb_s = c_prompt.shape[0], c_sample.shape[0]
    rows = -(-(nb_p + nb_s) // 8) * 8
    c_all = _pad_rows(jnp.concatenate([c_prompt, c_sample], 0), rows)
    mod_all = _ada(c_all, w_ada[0], b_ada[0]).reshape(rows, 6, d)
    modf_all = _ada(c_all, w_ada_final, b_ada_final).reshape(rows, 2, d)

    outs = []
    for x, lo, hi in ((x_prompt, 0, nb_p), (x_sample, nb_p, nb_p + nb_s)):
        outs.append(_encoder(x, mod_all[lo:hi], modf_all[lo:hi], wts))
    return tuple(outs)
```

```python
import functools
import math

import jax
import jax.numpy as jnp
from jax import lax
from jax.experimental import pallas as pl
from jax.experimental.pallas import tpu as pltpu

F32 = jnp.float32
BF16 = jnp.bfloat16

LANES = 128
SUBLANES_BF16 = 16
MXU_DIM = 256
VMEM_LIMIT_BYTES = 60 * 1024 * 1024

NORM_EPS = 1e-6
GN_EPS = 64e-5
GATED_EPS = 1e-5

HEAD = 64
PAIR = 2 * HEAD
RWKV_CHUNK = 64
SSD_CHUNK = 128
HALO_ROWS = 8
INPROJ_BLOCK = MXU_DIM
NEG_BIG = -1e30


def _cparams(sem):
    return pltpu.CompilerParams(dimension_semantics=sem, vmem_limit_bytes=VMEM_LIMIT_BYTES)


def _dot(a, b):
    return jnp.dot(a, b, preferred_element_type=F32)


def _dot_nt(a, b):
    return lax.dot_general(a, b, (((1,), (1,)), ((), ())), preferred_element_type=F32)


def _sigmoid(x):
    return 1.0 / (1.0 + jnp.exp(-x))


def _silu(x):
    return x * _sigmoid(x)


def _split3(x):
    hi = x.astype(BF16)
    r1 = x - hi.astype(F32)
    mid = r1.astype(BF16)
    lo = (r1 - mid.astype(F32)).astype(BF16)
    return hi, mid, lo


def _cumsum_rows(x, tri3, n):
    hi, mid, lo = _split3(x)
    stacked = jnp.concatenate([hi, mid, lo], axis=0)
    out = _dot(tri3, stacked)
    return out[:n], out[n:n + 1]


def _make_tri3(n, reverse):
    rows = lax.broadcasted_iota(jnp.int32, (n + 8, 3 * n), 0)
    cols = lax.broadcasted_iota(jnp.int32, (n + 8, 3 * n), 1) & (n - 1)
    tri = (cols >= rows) if reverse else (cols <= rows)
    return jnp.where(tri | (rows >= n), 1.0, 0.0).astype(BF16)


def _segsum_heads(x, bd):
    hi = x.astype(BF16)
    lo = (x - hi.astype(F32)).astype(BF16)
    outs = []
    for i in range(x.shape[1] // MXU_DIM):
        sl = slice(i * MXU_DIM, (i + 1) * MXU_DIM)
        outs.append(_dot(hi[:, sl], bd) + _dot(lo[:, sl], bd))
    return jnp.concatenate(outs, axis=1)


def _rms_mod(x, w, scale, shift):
    ms = jnp.mean(x * x, axis=-1, keepdims=True)
    return x * lax.rsqrt(ms + NORM_EPS) * w * (1.0 + scale) + shift


def _ada_kernel(c_ref, w_ref, b_ref, o_ref):
    c = c_ref[...]
    o_ref[...] = _dot(_silu(c), w_ref[...]) + b_ref[...]


def _ada(c_pad, w, b):
    rows, d = c_pad.shape
    n = w.shape[1]
    tn = 1024
    return pl.pallas_call(
        _ada_kernel,
        grid=(n // tn,),
        in_specs=[
            pl.BlockSpec((rows, d), lambda j: (0, 0)),
            pl.BlockSpec((d, tn), lambda j: (0, j)),
            pl.BlockSpec((1, tn), lambda j: (0, j)),
        ],
        out_specs=pl.BlockSpec((rows, tn), lambda j: (0, j)),
        out_shape=jax.ShapeDtypeStruct((rows, n), F32),
        compiler_params=_cparams(("arbitrary",)),
        name="ada_mod",
    )(c_pad, w, b.reshape(1, n))


def _inproj_kernel(tm, n_tiles, n_rw, width, x_ref, xp_ref, xn_ref, mod_ref, nw_ref, w_ref, mu_ref,
                   cw_ref, cb_ref, orw_ref, ossm_ref):
    j = pl.program_id(1)
    nw, scale, shift = nw_ref[...], mod_ref[0, 1:2, :], mod_ref[0, 0:1, :]
    h_prev = jnp.where(j > 0, _rms_mod(xp_ref[0], nw, scale, shift), 0.0)
    h_next = jnp.where(j < n_tiles - 1, _rms_mod(xn_ref[0], nw, scale, shift), 0.0)
    h = jnp.concatenate([h_prev, _rms_mod(x_ref[0], nw, scale, shift), h_next], axis=0).astype(BF16)
    rows = tm + 2 * HALO_ROWS
    mid = slice(HALO_ROWS, HALO_ROWS + tm)

    conv_k, conv_ch = cw_ref.shape
    left = (conv_k - 1) // 2
    n_ssm = ossm_ref.shape[2]

    def blocks(lo, hi):
        return [(c, min(c + INPROJ_BLOCK, hi)) for c in range(lo, hi, INPROJ_BLOCK)]

    def shift_tail(p, c0, c1):
        p_prev = pltpu.roll(p, 1, 0)
        p_next = pltpu.roll(p, rows - 1, 0)
        xs = p + mu_ref[0:1, c0:c1] * (p_prev - p) + mu_ref[1:2, c0:c1] * (p_next - p)
        orw_ref[0, :, c0:c1] = xs[mid]

    def copy_tail(p, c0, c1):
        ossm_ref[0, :, c0:c1] = p[mid]

    def conv_tail(ext, c0, c1):
        acc = cb_ref[:, c0:c1] + ext[mid] * cw_ref[left:left + 1, c0:c1]
        for i in range(conv_k):
            if i != left:
                shifted = pltpu.roll(ext, (left - i) % rows, 0)
                acc = acc + shifted[mid] * cw_ref[i:i + 1, c0:c1]
        ossm_ref[0, :, width + c0:width + c1] = _silu(acc)

    work = [(shift_tail, 0, c0, c1) for c0, c1 in blocks(0, n_rw)]
    work += [(conv_tail, n_rw + width, c0, c1) for c0, c1 in blocks(0, conv_ch)]
    work += [(copy_tail, n_rw, c0, c1) for c0, c1 in blocks(0, width) + blocks(width + conv_ch, n_ssm)]
    for tail, base, c0, c1 in work:
        tail(_dot(h, w_ref[:, base + c0:base + c1]), c0, c1)


def _in_proj(x, mod, norm_w, w_cat, mu, conv_w, conv_b, n_rw, width, tm):
    b, t, d = x.shape
    n_ssm = w_cat.shape[1] - n_rw
    n_tiles = t // tm
    hb = tm // HALO_ROWS
    last_hb = t // HALO_ROWS - 1
    return pl.pallas_call(
        functools.partial(_inproj_kernel, tm, n_tiles, n_rw, width),
        grid=(b, n_tiles),
        in_specs=[
            pl.BlockSpec((1, tm, d), lambda i, j: (i, j, 0)),
            pl.BlockSpec((1, HALO_ROWS, d), lambda i, j: (i, jnp.maximum(j * hb - 1, 0), 0)),
            pl.BlockSpec((1, HALO_ROWS, d), lambda i, j: (i, jnp.minimum((j + 1) * hb, last_hb), 0)),
            pl.BlockSpec((1, 6, d), lambda i, j: (i, 0, 0)),
            pl.BlockSpec((1, d), lambda i, j: (0, 0)),
            pl.BlockSpec(memory_space=pltpu.VMEM),
            pl.BlockSpec(mu.shape, lambda i, j: (0, 0)),
            pl.BlockSpec(conv_w.shape, lambda i, j: (0, 0)),
            pl.BlockSpec(conv_b.shape, lambda i, j: (0, 0)),
        ],
        out_specs=[
            pl.BlockSpec((1, tm, n_rw), lambda i, j: (i, j, 0)),
            pl.BlockSpec((1, tm, n_ssm), lambda i, j: (i, j, 0)),
        ],
        out_shape=[
            jax.ShapeDtypeStruct((b, t, n_rw), F32),
            jax.ShapeDtypeStruct((b, t, n_ssm), F32),
        ],
        compiler_params=_cparams(("arbitrary", "arbitrary")),
        name="in_proj",
    )(x, x, x, mod, norm_w.reshape(1, d), w_cat, mu, conv_w, conv_b)


def _stack2(x, m0):
    return jnp.concatenate([jnp.where(m0, x, 0.0), jnp.where(m0, 0.0, x)], axis=0)


def _rwkv_kernel(reverse, nb, width, *refs):
    if reverse:
        (xs_ref, w0_ref, w2_ref, a0_ref, a2_ref, kk_ref, ka_ref, bd_ref, y_ref, st_ref) = refs
    else:
        (xs_ref, w0_ref, w2_ref, a0_ref, a2_ref, kk_ref, ka_ref, bd_ref,
         a0o_ref, a2o_ref, g2_ref, rk_ref, lw_ref, lb_ref, yb_ref, y_ref, st_ref) = refs
    n = RWKV_CHUNK
    n_pairs = width // PAIR

    @pl.when(pl.program_id(1) == 0)
    def _():
        st_ref[...] = jnp.zeros_like(st_ref)

    bd = bd_ref[...]
    k_a = ka_ref[...]
    tri3 = _make_tri3(n, reverse)
    o = 3 * width

    def prep(bi):
        r = xs_ref[bi, :, 0:width]
        k = xs_ref[bi, :, width:2 * width]
        v = xs_ref[bi, :, 2 * width:o]
        wd = xs_ref[bi, :, o:o + LANES]
        ad_b = xs_ref[bi, :, o + LANES:o + 2 * LANES].astype(BF16)
        w_log = w0_ref[...] + _dot(jnp.tanh(wd).astype(BF16), w2_ref[...])
        lw = -math.exp(-0.5) * _sigmoid(w_log)
        a = _sigmoid(a0_ref[...] + _dot(ad_b, a2_ref[...]))
        kk = k * kk_ref[...]
        kk = kk / jnp.maximum(jnp.sqrt(_segsum_heads(kk * kk, bd)), 1e-12)
        k_d = k * (1.0 + (a - 1.0) * k_a)
        beta = kk * a
        c, ctot = _cumsum_rows(lw, tri3, n)
        e_nc = jnp.exp(-c)
        e_end = jnp.exp(ctot - c)
        return dict(
            r=r, k=k, v=v, ad_b=ad_b, k_d=k_d,
            a_t=-kk * jnp.exp(c - lw), r_t=r * jnp.exp(c), b_t=beta * e_nc, k_t=k_d * e_nc,
            b_h=beta * e_end, k_h=k_d * e_end, p_end=jnp.exp(ctot))

    pre = [prep(bi) for bi in range(nb)]

    lane = lax.broadcasted_iota(jnp.int32, (1, PAIR), 1)
    m0 = lane < HEAD
    t_idx = lax.broadcasted_iota(jnp.int32, (n, 2 * n), 0)
    col = lax.broadcasted_iota(jnp.int32, (n, 2 * n), 1)
    s_idx = col & (n - 1)
    col_h0 = col < n
    if reverse:
        strict, incl = s_idx > t_idx, s_idx >= t_idx
    else:
        strict, incl = s_idx < t_idx, s_idx <= t_idx
    incl2 = jnp.concatenate([incl, incl], axis=1)
    eye2 = jnp.where(s_idx == t_idx, 1.0, 0.0)
    br = lax.broadcasted_iota(jnp.int32, (PAIR, PAIR), 0) < HEAD
    bc = lax.broadcasted_iota(jnp.int32, (PAIR, PAIR), 1) < HEAD
    blk = br == bc

    def blockdiag(x):
        return jnp.concatenate(
            [jnp.where(col_h0, x, 0.0), jnp.where(col_h0, 0.0, x)], axis=0).astype(BF16)

    units = [(bi, j) for bi in range(nb) for j in range(n_pairs)]
    ids = range(len(units))

    def col_of(name, u):
        bi, j = units[u]
        return pre[bi][name][:, j * PAIR:(j + 1) * PAIR]

    g1s, g2s = [], []
    for u in ids:
        lhs = jnp.concatenate([col_of("a_t", u), col_of("r_t", u)], axis=0).astype(BF16)
        rhs = jnp.concatenate([_stack2(col_of("b_t", u), m0), _stack2(col_of("k_t", u), m0)],
                              axis=0).astype(BF16)
        g1s.append(_dot_nt(lhs, rhs))
        g2s.append(_dot_nt(lhs, st_ref[u].astype(BF16)))
    a_ab = [jnp.where(strict, g1s[u][:n, :2 * n], 0.0) for u in ids]
    tmat = [eye2 + a_ab[u] for u in ids]
    pw = a_ab
    for _ in range(int(math.log2(n)) - 1):
        pw = [_dot(pw[u].astype(BF16), blockdiag(pw[u])) for u in ids]
        tmat = [tmat[u] + _dot(tmat[u].astype(BF16), blockdiag(pw[u])) for u in ids]
    v2 = [_stack2(col_of("v", u), m0).astype(BF16) for u in ids]
    w_rhs = [g2s[u][:n] + _dot(jnp.where(strict, g1s[u][:n, 2 * n:], 0.0).astype(BF16), v2[u])
             for u in ids]
    us = [_dot(tmat[u].astype(BF16), _stack2(w_rhs[u], m0).astype(BF16)) for u in ids]
    ys = []
    for u in ids:
        a_r = jnp.where(incl2, g1s[u][n:], 0.0).astype(BF16)
        uv2 = jnp.concatenate([_stack2(us[u], m0).astype(BF16), v2[u]], axis=0)
        ys.append(g2s[u][n:] + _dot(a_r, uv2))
    for u in ids:
        uv_t = jnp.concatenate([us[u], col_of("v", u)], axis=0).T.astype(BF16)
        bk = jnp.concatenate([col_of("b_h", u), col_of("k_h", u)], axis=0).astype(BF16)
        st_ref[u] = st_ref[u] * col_of("p_end", u) + jnp.where(blk, _dot(uv_t, bk), 0.0)

    for bi in range(nb):
        y_all = jnp.concatenate(ys[bi * n_pairs:(bi + 1) * n_pairs], axis=1)
        if reverse:
            y_ref[bi] = y_all
        else:
            pb = pre[bi]
            ysum = y_all + yb_ref[bi]
            inv = 1.0 / HEAD
            mean = _segsum_heads(ysum, bd) * inv
            dlt = ysum - mean
            var = _segsum_heads(dlt * dlt, bd) * inv
            yn = dlt * lax.rsqrt(var + GN_EPS) * lw_ref[...] + lb_ref[...]
            a_o = _sigmoid(a0o_ref[...] + _dot(pb["ad_b"], a2o_ref[...]))
            ksum = pb["k_d"] + pb["k"] * (1.0 + (a_o - 1.0) * k_a)
            bonus = _segsum_heads(pb["r"] * ksum * rk_ref[...], bd) * pb["v"]
            gd = xs_ref[bi, :, o + 2 * LANES:o + 3 * LANES]
            g = _dot(_sigmoid(gd).astype(BF16), g2_ref[...])
            y_ref[bi] = ((yn + bonus) * g).astype(y_ref.dtype)


def _rwkv_scan(reverse, xs, consts, nb, yb=None):
    b, t, ncol = xs.shape
    width = consts["width"]
    n = RWKV_CHUNK
    n_chunks = t // n

    def cidx(j):
        return (n_chunks - 1 - j) if reverse else j

    def row(x):
        return pl.BlockSpec((1, x.shape[1]), lambda i, j: (0, 0))

    def full(x):
        return pl.BlockSpec(x.shape, lambda i, j: (0,) * x.ndim)

    d = 1 if reverse else 0
    args = [xs, consts["w0"][d], consts["w2"][d], consts["a0"][d], consts["a2"][d], consts["k_k"],
            consts["k_a"], consts["bd"]]
    specs = [
        pl.BlockSpec((nb, n, ncol), lambda i, j: (i, cidx(j), 0)),
        row(consts["w0"][d]), full(consts["w2"][d]), row(consts["a0"][d]),
        full(consts["a2"][d]), row(consts["k_k"]), row(consts["k_a"]), full(consts["bd"]),
    ]
    if not reverse:
        args += [consts["a0"][1], consts["a2"][1], consts["g2"], consts["r_k"], consts["lnx_w"],
                 consts["lnx_b"], yb]
        specs += [row(consts["a0"][1]), full(consts["a2"][1]), full(consts["g2"]), row(consts["r_k"]),
                  row(consts["lnx_w"]), row(consts["lnx_b"]),
                  pl.BlockSpec((nb, n, width), lambda i, j: (i, cidx(j), 0))]
    out_dtype = F32 if reverse else BF16
    return pl.pallas_call(
        functools.partial(_rwkv_kernel, reverse, nb, width),
        grid=(b // nb, n_chunks),
        in_specs=specs,
        out_specs=pl.BlockSpec((nb, n, width), lambda i, j: (i, cidx(j), 0)),
        out_shape=jax.ShapeDtypeStruct((b, t, width), out_dtype),
        scratch_shapes=[pltpu.VMEM((nb * (width // PAIR), PAIR, PAIR), F32)],
        compiler_params=_cparams(("arbitrary", "arbitrary")),
        name="rwkv_bwd" if reverse else "rwkv_fwd",
    )(*args)


def _ssd_kernel(reverse, nb, width, n_state, *refs):
    if reverse:
        (p_ref, dtb_ref, alog_ref, ed_ref, y_ref, st_ref) = refs
    else:
        (p_ref, dtb_ref, alog_ref, ed_ref, dsk_ref, nw_ref, yb_ref, y_ref, st_ref) = refs
    q = SSD_CHUNK
    n_groups = 2
    conv_ch = width + 2 * n_groups * n_state
    c1 = width + conv_ch
    heads = width // HEAD
    n_pairs = heads // 2
    pairs_per_group = n_pairs // n_groups
    d = 1 if reverse else 0

    @pl.when(pl.program_id(1) == 0)
    def _():
        st_ref[...] = jnp.zeros_like(st_ref)

    tri3 = _make_tri3(q, reverse)
    ed = ed_ref[...]
    neg_a = -jnp.exp(alog_ref[...])

    def expand(x):
        hi, mid, lo = _split3(x)
        return _dot(hi, ed) + _dot(mid, ed) + _dot(lo, ed)

    def prep(bi):
        z_dt = p_ref[bi, :, c1:c1 + LANES] + dtb_ref[...]
        dt = jnp.maximum(z_dt, 0.0) + jnp.log(1.0 + jnp.exp(-jnp.abs(z_dt)))
        acs, atot = _cumsum_rows(dt * neg_a, tri3, q)
        dt_x = expand(dt)
        acs_x = expand(jnp.concatenate([acs, jnp.broadcast_to(atot, (HALO_ROWS, LANES))], axis=0))
        atot_x = acs_x[q:q + 1]
        acs_x = acs_x[:q]
        xs = p_ref[bi, :, width:2 * width]
        xd = xs * dt_x
        return dict(acs=acs, acs_t=acs.T, xs=xs, xd=xd, e_acs=jnp.exp(acs_x),
                    xdec=xd * jnp.exp(atot_x - acs_x), p_end=jnp.exp(atot_x))

    pre = [prep(bi) for bi in range(nb)]

    t_idx = lax.broadcasted_iota(jnp.int32, (q, q), 0)
    s_idx = lax.broadcasted_iota(jnp.int32, (q, q), 1)
    incl = (s_idx >= t_idx) if reverse else (s_idx <= t_idx)
    lane = lax.broadcasted_iota(jnp.int32, (1, PAIR), 1)
    m0 = lane < HEAD

    groups = [(bi, g) for bi in range(nb) for g in range(n_groups)]
    cms, bmts, cbs = [], [], []
    for bi, g in groups:
        b0 = 2 * width + g * n_state
        c0 = 2 * width + (n_groups + g) * n_state
        bm = p_ref[bi, :, b0:b0 + n_state]
        cm_b = p_ref[bi, :, c0:c0 + n_state].astype(BF16)
        cms.append(cm_b)
        cbs.append(_dot_nt(cm_b, bm.astype(BF16)))
        bmts.append(bm.T.astype(BF16))
    units = [(bi, j) for bi in range(nb) for j in range(n_pairs)]
    ys = []
    for u, (bi, j) in enumerate(units):
        pb = pre[bi]
        gi = bi * n_groups + j // pairs_per_group
        sl = slice(j * PAIR, (j + 1) * PAIR)
        ms = []
        for hh in range(2):
            li = d * heads + 2 * j + hh
            diff = pb["acs"][:, li:li + 1] - pb["acs_t"][li:li + 1, :]
            ms.append((cbs[gi] * jnp.exp(jnp.where(incl, diff, NEG_BIG))).astype(BF16))
        y_diag = _dot(jnp.concatenate(ms, axis=1), _stack2(pb["xd"][:, sl], m0).astype(BF16))
        y_off = _dot(cms[gi], st_ref[u].astype(BF16)) * pb["e_acs"][:, sl]
        ys.append(y_diag + y_off)
    for u, (bi, j) in enumerate(units):
        pb = pre[bi]
        gi = bi * n_groups + j // pairs_per_group
        sl = slice(j * PAIR, (j + 1) * PAIR)
        st_ref[u] = st_ref[u] * pb["p_end"][:, sl] + _dot(bmts[gi], pb["xdec"][:, sl].astype(BF16))

    for bi in range(nb):
        y_all = jnp.concatenate(ys[bi * n_pairs:(bi + 1) * n_pairs], axis=1)
        gate = _silu(p_ref[bi, :, 0:width])
        if reverse:
            y_ref[bi] = y_all * gate
        else:
            y = (y_all + pre[bi]["xs"] * dsk_ref[...]) * gate + yb_ref[bi]
            ms = jnp.mean(y * y, axis=-1, keepdims=True)
            y_ref[bi] = (y * lax.rsqrt(ms + GATED_EPS) * nw_ref[...]).astype(y_ref.dtype)


def _ssd_scan(reverse, p_ssm, consts, nb, yb=None):
    b, t, ncol = p_ssm.shape
    width = consts["width"]
    n_state = consts["n_state"]
    q = SSD_CHUNK
    n_chunks = t // q

    def cidx(j):
        return (n_chunks - 1 - j) if reverse else j

    def row(x):
        return pl.BlockSpec((1, x.shape[1]), lambda i, j: (0, 0))

    def full(x):
        return pl.BlockSpec(x.shape, lambda i, j: (0,) * x.ndim)

    expand = consts["expand"][1 if reverse else 0]
    args = [p_ssm, consts["dt_bias"], consts["a_log"], expand]
    specs = [
        pl.BlockSpec((nb, q, ncol), lambda i, j: (i, cidx(j), 0)),
        row(consts["dt_bias"]), row(consts["a_log"]), full(expand),
    ]
    if not reverse:
        args += [consts["d_skip"], consts["norm_w"], yb]
        specs += [row(consts["d_skip"]), row(consts["norm_w"]),
                  pl.BlockSpec((nb, q, width), lambda i, j: (i, cidx(j), 0))]
    out_dtype = F32 if reverse else BF16
    return pl.pallas_call(
        functools.partial(_ssd_kernel, reverse, nb, width, n_state),
        grid=(b // nb, n_chunks),
        in_specs=specs,
        out_specs=pl.BlockSpec((nb, q, width), lambda i, j: (i, cidx(j), 0)),
        out_shape=jax.ShapeDtypeStruct((b, t, width), out_dtype),
        scratch_shapes=[pltpu.VMEM((nb * (width // PAIR), n_state, PAIR), F32)],
        compiler_params=_cparams(("arbitrary", "arbitrary")),
        name="ssd_bwd" if reverse else "ssd_fwd",
    )(*args)


def _outproj_kernel(half, x_ref, yr_ref, ys_ref, mod_ref, nw_ref, w_ref, x1_ref, h2_ref):
    attn = _dot(yr_ref[0], w_ref[:half, :]) + _dot(ys_ref[0], w_ref[half:, :])
    x1 = x_ref[0] + mod_ref[0, 2:3, :] * attn
    x1_ref[0] = x1
    h2_ref[0] = _rms_mod(x1, nw_ref[...], mod_ref[0, 4:5, :], mod_ref[0, 3:4, :]).astype(BF16)


def _out_proj(x, y_rw, y_ssm, mod, norm_w, w_out, tm):
    b, t, d = x.shape
    half = y_rw.shape[2]
    return pl.pallas_call(
        functools.partial(_outproj_kernel, half),
        grid=(b, t // tm),
        in_specs=[
            pl.BlockSpec((1, tm, d), lambda i, j: (i, j, 0)),
            pl.BlockSpec((1, tm, half), lambda i, j: (i, j, 0)),
            pl.BlockSpec((1, tm, half), lambda i, j: (i, j, 0)),
            pl.BlockSpec((1, 6, d), lambda i, j: (i, 0, 0)),
            pl.BlockSpec((1, d), lambda i, j: (0, 0)),
            pl.BlockSpec(memory_space=pltpu.VMEM),
        ],
        out_specs=[
            pl.BlockSpec((1, tm, d), lambda i, j: (i, j, 0)),
            pl.BlockSpec((1, tm, d), lambda i, j: (i, j, 0)),
        ],
        out_shape=[
            jax.ShapeDtypeStruct((b, t, d), F32),
            jax.ShapeDtypeStruct((b, t, d), BF16),
        ],
        compiler_params=_cparams(("arbitrary", "arbitrary")),
        name="out_proj",
    )(x, y_rw, y_ssm, mod, norm_w.reshape(1, d), w_out)


def _ffn_kernel(tm, n_tiles, n_ff, h_ref, hp_ref, hn_ref, x1_hbm, mod_ref, modf_ref, fw_ref,
                wv_ref, wg_ref, cwv_ref, cwg_ref, cbv_ref, cbg_ref, wd_ref,
                o_ref, lhs_ref, uv_ref, ug_ref, x1_ref, x1_sem):
    i = pl.program_id(1)
    f = pl.program_id(2)
    hal = SUBLANES_BF16
    x1_copy = pltpu.make_async_copy(
        x1_hbm.at[pl.program_id(0), pl.ds(i * tm, tm), :], x1_ref, x1_sem)

    @pl.when(f == 0)
    def _():
        x1_copy.start()
        lhs_ref[0:hal, :] = jnp.where(i > 0, hp_ref[0], jnp.zeros_like(hp_ref[0]))
        lhs_ref[hal:hal + tm, :] = h_ref[0]
        lhs_ref[hal + tm:, :] = jnp.where(i < n_tiles - 1, hn_ref[0], jnp.zeros_like(hn_ref[0]))
        o_ref[0] = jnp.zeros(o_ref.shape[1:], F32)

    lhs = lhs_ref[...]
    uv_ref[...] = _dot(lhs, wv_ref[...])
    ug_ref[...] = _dot(lhs, wg_ref[...])

    def conv(u_ref, cw_ref, cb_ref):
        out = cb_ref[...] + u_ref[pl.ds(hal - 1, tm), :] * cw_ref[0:1, :]
        out = out + u_ref[pl.ds(hal, tm), :] * cw_ref[1:2, :]
        return out + u_ref[pl.ds(hal + 1, tm), :] * cw_ref[2:3, :]

    act = _silu(conv(uv_ref, cwv_ref, cbv_ref)) * conv(ug_ref, cwg_ref, cbg_ref)
    o_ref[0] += _dot(act.astype(BF16), wd_ref[...])

    @pl.when(f == n_ff - 1)
    def _():
        x1_copy.wait()
        x2 = x1_ref[...] + mod_ref[0, 5:6, :] * o_ref[0]
        o_ref[0] = _rms_mod(x2, fw_ref[...], modf_ref[0, 1:2, :], modf_ref[0, 0:1, :])


def _ffn(h2, x1, mod, modf, final_w, w_up, conv_w, conv_b, w_down, tm, ck):
    b, t, d = x1.shape
    d_ff = w_down.shape[0]
    n_ff = d_ff // ck
    n_tiles = t // tm
    hal = SUBLANES_BF16
    hb = tm // hal
    last_hb = t // hal - 1
    cb2 = conv_b.reshape(1, 2 * d_ff)
    return pl.pallas_call(
        functools.partial(_ffn_kernel, tm, n_tiles, n_ff),
        grid=(b, n_tiles, n_ff),
        in_specs=[
            pl.BlockSpec((1, tm, d), lambda i, j, f: (i, j, 0)),
            pl.BlockSpec((1, hal, d), lambda i, j, f: (i, jnp.maximum(j * hb - 1, 0), 0)),
            pl.BlockSpec((1, hal, d), lambda i, j, f: (i, jnp.minimum((j + 1) * hb, last_hb), 0)),
            pl.BlockSpec(memory_space=pl.ANY),
            pl.BlockSpec((1, 6, d), lambda i, j, f: (i, 0, 0)),
            pl.BlockSpec((1, 2, d), lambda i, j, f: (i, 0, 0)),
            pl.BlockSpec((1, d), lambda i, j, f: (0, 0)),
            pl.BlockSpec((d, ck), lambda i, j, f: (0, f)),
            pl.BlockSpec((d, ck), lambda i, j, f: (0, f + n_ff)),
            pl.BlockSpec((conv_w.shape[0], ck), lambda i, j, f: (0, f)),
            pl.BlockSpec((conv_w.shape[0], ck), lambda i, j, f: (0, f + n_ff)),
            pl.BlockSpec((1, ck), lambda i, j, f: (0, f)),
            pl.BlockSpec((1, ck), lambda i, j, f: (0, f + n_ff)),
            pl.BlockSpec((ck, d), lambda i, j, f: (f, 0)),
        ],
        out_specs=pl.BlockSpec((1, tm, d), lambda i, j, f: (i, j, 0)),
        out_shape=jax.ShapeDtypeStruct((b, t, d), F32),
        scratch_shapes=[
            pltpu.VMEM((tm + 2 * hal, d), BF16),
            pltpu.VMEM((tm + 2 * hal, ck), F32),
            pltpu.VMEM((tm + 2 * hal, ck), F32),
            pltpu.VMEM((tm, d), F32),
            pltpu.SemaphoreType.DMA(()),
        ],
        compiler_params=_cparams(("arbitrary", "arbitrary", "arbitrary")),
        name="conv_glu_ffn",
    )(h2, h2, h2, x1, mod, modf, final_w.reshape(1, d), w_up, w_up, conv_w, conv_w, cb2, cb2, w_down)


def _pad_rows(x, rows):
    return jnp.concatenate([x, jnp.zeros((rows - x.shape[0],) + x.shape[1:], x.dtype)], axis=0)


def _pad_lanes(x, lanes):
    return jnp.concatenate([x, jnp.zeros(x.shape[:-1] + (lanes - x.shape[-1],), x.dtype)], axis=-1)


def _encoder(x, mod, modf, wts):
    rw, ssm = wts["rwkv"], wts["ssm"]
    xs_rw, p_ssm = _in_proj(x, mod, wts["norm1_w"], wts["w_in"], rw["mu"], ssm["conv_w"],
                            ssm["conv_b"], rw["ncol"], rw["width"], tm=256)
    nb = 2 if x.shape[0] % 2 == 0 else 1
    yb = _rwkv_scan(True, xs_rw, rw, nb)
    y_rw = _rwkv_scan(False, xs_rw, rw, nb, yb)
    sb = _ssd_scan(True, p_ssm, ssm, nb)
    y_ssm = _ssd_scan(False, p_ssm, ssm, nb, sb)
    x1, h2 = _out_proj(x, y_rw, y_ssm, mod, wts["norm2_w"], wts["w_out"], tm=512)
    return _ffn(h2, x1, mod, modf, wts["final_norm_w"], wts["w_up"], wts["ffn_conv_w"],
                wts["ffn_conv_b"], wts["w_down"], tm=1024, ck=256)


def kernel(x_prompt, x_sample, c_prompt, c_sample, norm1_w, w_in, rwkv_mu, rwkv_w0, rwkv_w2, rwkv_a0, rwkv_a2, rwkv_g2, rwkv_k_k, rwkv_k_a, rwkv_r_k, rwkv_lnx_w, rwkv_lnx_b, ssm_conv_w, ssm_conv_b, ssm_dt_bias, ssm_a_log, ssm_d, ssm_norm_w, w_out, norm2_w, ffn_w_up, ffn_conv_w, ffn_conv_b, ffn_w_down, w_ada, b_ada, final_norm_w, w_ada_final, b_ada_final):
    assert w_in.shape[0] == 1, "single-layer trunk"
    d = x_prompt.shape[-1]
    width = rwkv_w0.shape[-1]
    heads = width // HEAD
    lora = rwkv_w2.shape[2]
    n_rw = rwkv_mu.shape[-1]
    n_state = (ssm_conv_w.shape[-1] - width) // 4
    assert 2 * lora == LANES and rwkv_g2.shape[1] == LANES and 2 * heads <= LANES

    n_ssm = w_in.shape[2] - n_rw
    n_ssm_pad = -(-n_ssm // LANES) * LANES
    w_cat = _pad_lanes(w_in[0], n_rw + n_ssm_pad).astype(BF16)
    zl = jnp.zeros((lora, width), F32)
    eye_h = jnp.repeat(jnp.eye(heads, dtype=F32), HEAD, axis=1)
    seg = jnp.arange(MXU_DIM) // HEAD
    rwkv = {
        "width": width, "ncol": n_rw,
        "mu": rwkv_mu[0],
        "w0": [rwkv_w0[0, i:i + 1] for i in range(2)],
        "a0": [rwkv_a0[0, i:i + 1] for i in range(2)],
        "w2": [jnp.concatenate([rwkv_w2[0, 0], zl], 0).astype(BF16),
               jnp.concatenate([zl, rwkv_w2[0, 1]], 0).astype(BF16)],
        "a2": [jnp.concatenate([rwkv_a2[0, 0], zl], 0).astype(BF16),
               jnp.concatenate([zl, rwkv_a2[0, 1]], 0).astype(BF16)],
        "g2": rwkv_g2[0].astype(BF16),
        "k_k": rwkv_k_k, "k_a": rwkv_k_a, "r_k": rwkv_r_k[0].reshape(1, width),
        "lnx_w": rwkv_lnx_w, "lnx_b": rwkv_lnx_b,
        "bd": (seg[:, None] == seg[None, :]).astype(BF16),
    }
    ssm = {
        "width": width, "n_state": n_state,
        "conv_w": ssm_conv_w[0], "conv_b": ssm_conv_b,
        "dt_bias": _pad_lanes(ssm_dt_bias[0].reshape(1, 2 * heads), LANES),
        "a_log": _pad_lanes(ssm_a_log[0].reshape(1, 2 * heads), LANES),
        "d_skip": jnp.repeat(ssm_d[0], HEAD).reshape(1, width),
        "norm_w": ssm_norm_w,
        "expand": [_pad_rows(jnp.concatenate([eye_h * (1 - i), eye_h * i], 0), LANES).astype(BF16)
                   for i in range(2)],
    }
    wts = {
        "norm1_w": norm1_w[0], "norm2_w": norm2_w[0], "final_norm_w": final_norm_w,
        "w_in": w_cat, "w_out": w_out[0].astype(BF16),
        "w_up": ffn_w_up[0].astype(BF16), "w_down": ffn_w_down[0].astype(BF16),
        "ffn_conv_w": ffn_conv_w[0], "ffn_conv_b": ffn_conv_b[0],
        "rwkv": rwkv, "ssm": ssm,
    }

    nb_p, nb_s = c_prompt.shape[0], c_sample.shape[0]
    rows = -(-(nb_p + nb_s) // 8) * 8
    c_all = _pad_rows(jnp.concatenate([c_prompt, c_sample], 0), rows)
    mod_all = _ada(c_all, w_ada[0], b_ada[0]).reshape(rows, 6, d)
    modf_all = _ada(c_all, w_ada_final, b_ada_final).reshape(rows, 2, d)

    outs = []
    for x, lo, hi in ((x_prompt, 0, nb_p), (x_sample, nb_p, nb_p + nb_s)):
        outs.append(_encoder(x, mod_all[lo:hi], modf_all[lo:hi], wts))
    return tuple(outs)
```

```python
import functools
import math

import jax
import jax.numpy as jnp
from jax import lax
from jax.experimental import pallas as pl
from jax.experimental.pallas import tpu as pltpu

F32 = jnp.float32
BF16 = jnp.bfloat16

LANES = 128
SUBLANES_BF16 = 16
MXU_DIM = 256
VMEM_LIMIT_BYTES = 60 * 1024 * 1024

NORM_EPS = 1e-6
GN_EPS = 64e-5
GATED_EPS = 1e-5

HEAD = 64
PAIR = 2 * HEAD
RWKV_CHUNK = 64
SSD_CHUNK = 128
SCAN_ROWS = 4
HALO_ROWS = 8
INPROJ_BLOCK = MXU_DIM
FFN_SUB = MXU_DIM
NEG_BIG = -1e30


def _cparams(sem):
    return pltpu.CompilerParams(dimension_semantics=sem, vmem_limit_bytes=VMEM_LIMIT_BYTES)


def _dot(a, b):
    return jnp.dot(a, b, preferred_element_type=F32)


def _dot_nt(a, b):
    return lax.dot_general(a, b, (((1,), (1,)), ((), ())), preferred_element_type=F32)


def _sigmoid(x):
    return 1.0 / (1.0 + jnp.exp(-x))


def _silu(x):
    return x * _sigmoid(x)


def _split3(x):
    hi = x.astype(BF16)
    r1 = x - hi.astype(F32)
    mid = r1.astype(BF16)
    lo = (r1 - mid.astype(F32)).astype(BF16)
    return hi, mid, lo


def _cumsum_rows(x, tri3, n):
    hi, mid, lo = _split3(x)
    stacked = jnp.concatenate([hi, mid, lo], axis=0)
    out = _dot(tri3, stacked)
    return out[:n], out[n:n + 1]


def _make_tri3(n, reverse):
    rows = lax.broadcasted_iota(jnp.int32, (n + 8, 3 * n), 0)
    cols = lax.broadcasted_iota(jnp.int32, (n + 8, 3 * n), 1) & (n - 1)
    tri = (cols >= rows) if reverse else (cols <= rows)
    return jnp.where(tri | (rows >= n), 1.0, 0.0).astype(BF16)


def _segsum_heads(xs, bd):
    n = xs[0].shape[0]
    parts = []
    for x in xs:
        hi = x.astype(BF16)
        parts += [hi, (x - hi.astype(F32)).astype(BF16)]
    stacked = jnp.concatenate(parts, axis=0)
    out = jnp.concatenate(
        [_dot(stacked[:, i * MXU_DIM:(i + 1) * MXU_DIM], bd)
         for i in range(stacked.shape[1] // MXU_DIM)], axis=1)
    return [out[2 * i * n:(2 * i + 1) * n] + out[(2 * i + 1) * n:(2 * i + 2) * n]
            for i in range(len(xs))]


def _rms_mod(x, w, scale, shift):
    ms = jnp.mean(x * x, axis=-1, keepdims=True)
    return x * lax.rsqrt(ms + NORM_EPS) * w * (1.0 + scale) + shift


def _ada_kernel(c_ref, w_ref, b_ref, o_ref):
    c = c_ref[...]
    o_ref[...] = _dot(_silu(c), w_ref[...]) + b_ref[...]


def _ada(c_pad, w, b):
    rows, d = c_pad.shape
    n = w.shape[1]
    tn = 1024
    return pl.pallas_call(
        _ada_kernel,
        grid=(n // tn,),
        in_specs=[
            pl.BlockSpec((rows, d), lambda j: (0, 0)),
            pl.BlockSpec((d, tn), lambda j: (0, j)),
            pl.BlockSpec((1, tn), lambda j: (0, j)),
        ],
        out_specs=pl.BlockSpec((rows, tn), lambda j: (0, j)),
        out_shape=jax.ShapeDtypeStruct((rows, n), F32),
        compiler_params=_cparams(("arbitrary",)),
        name="ada_mod",
    )(c_pad, w, b.reshape(1, n))


def _inproj_kernel(tm, n_tiles, n_rw, width, x_ref, xp_ref, xn_ref, mod_ref, nw_ref, w_ref, mu_ref,
                   cw_ref, cb_ref, orw_ref, ossm_ref):
    j = pl.program_id(1)
    nw, scale, shift = nw_ref[...], mod_ref[0, 1:2, :], mod_ref[0, 0:1, :]
    h_prev = jnp.where(j > 0, _rms_mod(xp_ref[0], nw, scale, shift), 0.0)
    h_next = jnp.where(j < n_tiles - 1, _rms_mod(xn_ref[0], nw, scale, shift), 0.0)
    h = jnp.concatenate([h_prev, _rms_mod(x_ref[0], nw, scale, shift), h_next], axis=0).astype(BF16)
    rows = tm + 2 * HALO_ROWS
    mid = slice(HALO_ROWS, HALO_ROWS + tm)

    conv_k, conv_ch = cw_ref.shape
    left = (conv_k - 1) // 2
    n_ssm = ossm_ref.shape[2]

    def blocks(lo, hi):
        return [(c, min(c + INPROJ_BLOCK, hi)) for c in range(lo, hi, INPROJ_BLOCK)]

    def shift_tail(p, c0, c1):
        p_prev = pltpu.roll(p, 1, 0)
        p_next = pltpu.roll(p, rows - 1, 0)
        xs = p + mu_ref[0:1, c0:c1] * (p_prev - p) + mu_ref[1:2, c0:c1] * (p_next - p)
        orw_ref[0, :, c0:c1] = xs[mid]

    def copy_tail(p, c0, c1):
        ossm_ref[0, :, c0:c1] = p[mid]

    def conv_tail(ext, c0, c1):
        acc = cb_ref[:, c0:c1] + ext[mid] * cw_ref[left:left + 1, c0:c1]
        for i in range(conv_k):
            if i != left:
                shifted = pltpu.roll(ext, (left - i) % rows, 0)
                acc = acc + shifted[mid] * cw_ref[i:i + 1, c0:c1]
        ossm_ref[0, :, width + c0:width + c1] = _silu(acc)

    work = [(shift_tail, 0, c0, c1) for c0, c1 in blocks(0, n_rw)]
    work += [(conv_tail, n_rw + width, c0, c1) for c0, c1 in blocks(0, conv_ch)]
    work += [(copy_tail, n_rw, c0, c1) for c0, c1 in blocks(0, width) + blocks(width + conv_ch, n_ssm)]
    for tail, base, c0, c1 in work:
        tail(_dot(h, w_ref[:, base + c0:base + c1]), c0, c1)


def _in_proj(x, mod, norm_w, w_cat, mu, conv_w, conv_b, n_rw, width, tm):
    b, t, d = x.shape
    n_ssm = w_cat.shape[1] - n_rw
    n_tiles = t // tm
    hb = tm // HALO_ROWS
    last_hb = t // HALO_ROWS - 1
    return pl.pallas_call(
        functools.partial(_inproj_kernel, tm, n_tiles, n_rw, width),
        grid=(b, n_tiles),
        in_specs=[
            pl.BlockSpec((1, tm, d), lambda i, j: (i, j, 0)),
            pl.BlockSpec((1, HALO_ROWS, d), lambda i, j: (i, jnp.maximum(j * hb - 1, 0), 0)),
            pl.BlockSpec((1, HALO_ROWS, d), lambda i, j: (i, jnp.minimum((j + 1) * hb, last_hb), 0)),
            pl.BlockSpec((1, 6, d), lambda i, j: (i, 0, 0)),
            pl.BlockSpec((1, d), lambda i, j: (0, 0)),
            pl.BlockSpec(memory_space=pltpu.VMEM),
            pl.BlockSpec(mu.shape, lambda i, j: (0, 0)),
            pl.BlockSpec(conv_w.shape, lambda i, j: (0, 0)),
            pl.BlockSpec(conv_b.shape, lambda i, j: (0, 0)),
        ],
        out_specs=[
            pl.BlockSpec((1, tm, n_rw), lambda i, j: (i, j, 0)),
            pl.BlockSpec((1, tm, n_ssm), lambda i, j: (i, j, 0)),
        ],
        out_shape=[
            jax.ShapeDtypeStruct((b, t, n_rw), F32),
            jax.ShapeDtypeStruct((b, t, n_ssm), F32),
        ],
        compiler_params=_cparams(("arbitrary", "arbitrary")),
        name="in_proj",
    )(x, x, x, mod, norm_w.reshape(1, d), w_cat, mu, conv_w, conv_b)


def _stack2(x, m0):
    return jnp.concatenate([jnp.where(m0, x, 0.0), jnp.where(m0, 0.0, x)], axis=0)


def _rwkv_kernel(reverse, nb, width, *refs):
    if reverse:
        (xs_ref, w0_ref, w2_ref, a0_ref, a2_ref, kk_ref, ka_ref, bd_ref, y_ref, st_ref) = refs
    else:
        (xs_ref, w0_ref, w2_ref, a0_ref, a2_ref, kk_ref, ka_ref, bd_ref,
         a0o_ref, a2o_ref, g2_ref, rk_ref, lw_ref, lb_ref, yb_ref, y_ref, st_ref) = refs
    n = RWKV_CHUNK
    n_pairs = width // PAIR

    @pl.when(pl.program_id(1) == 0)
    def _():
        st_ref[...] = jnp.zeros_like(st_ref)

    bd = bd_ref[...]
    k_a = ka_ref[...]
    tri3 = _make_tri3(n, reverse)
    o = 3 * width

    def cols(c0, c1):
        return xs_ref[:, :, c0:c1].reshape(nb * n, c1 - c0)

    r, k, v = cols(0, width), cols(width, 2 * width), cols(2 * width, o)
    ad_b = cols(o + LANES, o + 2 * LANES).astype(BF16)
    w_log = w0_ref[...] + _dot(jnp.tanh(cols(o, o + LANES)).astype(BF16), w2_ref[...])
    lw = -math.exp(-0.5) * _sigmoid(w_log)
    a = _sigmoid(a0_ref[...] + _dot(ad_b, a2_ref[...]))
    kk = k * kk_ref[...]
    k_d = k * (1.0 + (a - 1.0) * k_a)
    if reverse:
        (kk_sq,) = _segsum_heads([kk * kk], bd)
    else:
        a_o = _sigmoid(a0o_ref[...] + _dot(ad_b, a2o_ref[...]))
        ksum = k_d + k * (1.0 + (a_o - 1.0) * k_a)
        kk_sq, rk_sum = _segsum_heads([kk * kk, r * ksum * rk_ref[...]], bd)
    kk = kk / jnp.maximum(jnp.sqrt(kk_sq), 1e-12)
    beta = kk * a
    cums = [_cumsum_rows(lw[bi * n:(bi + 1) * n], tri3, n) for bi in range(nb)]
    c = jnp.concatenate([cu[0] for cu in cums], axis=0)
    ctot_rows = jnp.concatenate([jnp.broadcast_to(cu[1], (n, width)) for cu in cums], axis=0)
    e_nc = jnp.exp(-c)
    e_end = jnp.exp(ctot_rows - c)
    pre = dict(v=v, a_t=-kk * jnp.exp(c - lw), r_t=r * jnp.exp(c), b_t=beta * e_nc, k_t=k_d * e_nc,
               b_h=beta * e_end, k_h=k_d * e_end)
    p_end = [jnp.exp(cu[1]) for cu in cums]

    lane = lax.broadcasted_iota(jnp.int32, (1, PAIR), 1)
    m0 = lane < HEAD
    t_idx = lax.broadcasted_iota(jnp.int32, (n, 2 * n), 0)
    col = lax.broadcasted_iota(jnp.int32, (n, 2 * n), 1)
    s_idx = col & (n - 1)
    col_h0 = col < n
    if reverse:
        strict, incl = s_idx > t_idx, s_idx >= t_idx
    else:
        strict, incl = s_idx < t_idx, s_idx <= t_idx
    incl2 = jnp.concatenate([incl, incl], axis=1)
    eye2 = jnp.where(s_idx == t_idx, 1.0, 0.0)
    br = lax.broadcasted_iota(jnp.int32, (PAIR, PAIR), 0) < HEAD
    bc = lax.broadcasted_iota(jnp.int32, (PAIR, PAIR), 1) < HEAD
    blk = br == bc

    def blockdiag(x):
        return jnp.concatenate(
            [jnp.where(col_h0, x, 0.0), jnp.where(col_h0, 0.0, x)], axis=0).astype(BF16)

    units = [(bi, j) for bi in range(nb) for j in range(n_pairs)]
    ids = range(len(units))

    def col_of(name, u):
        bi, j = units[u]
        return pre[name][bi * n:(bi + 1) * n, j * PAIR:(j + 1) * PAIR]

    g1s, g2s = [], []
    for u in ids:
        lhs = jnp.concatenate([col_of("a_t", u), col_of("r_t", u)], axis=0).astype(BF16)
        rhs = jnp.concatenate([_stack2(col_of("b_t", u), m0), _stack2(col_of("k_t", u), m0)],
                              axis=0).astype(BF16)
        g1s.append(_dot_nt(lhs, rhs))
        g2s.append(_dot_nt(lhs, st_ref[u].astype(BF16)))
    a_ab = [jnp.where(strict, g1s[u][:n, :2 * n], 0.0) for u in ids]
    tmat = [eye2 + a_ab[u] for u in ids]
    pw = a_ab
    for _ in range(int(math.log2(n)) - 1):
        pw = [_dot(pw[u].astype(BF16), blockdiag(pw[u])) for u in ids]
        tmat = [tmat[u] + _dot(tmat[u].astype(BF16), blockdiag(pw[u])) for u in ids]
    v2 = [_stack2(col_of("v", u), m0).astype(BF16) for u in ids]
    w_rhs = [g2s[u][:n] + _dot(jnp.where(strict, g1s[u][:n, 2 * n:], 0.0).astype(BF16), v2[u])
             for u in ids]
    us = [_dot(tmat[u].astype(BF16), _stack2(w_rhs[u], m0).astype(BF16)) for u in ids]
    ys = []
    for u in ids:
        a_r = jnp.where(incl2, g1s[u][n:], 0.0).astype(BF16)
        uv2 = jnp.concatenate([_stack2(us[u], m0).astype(BF16), v2[u]], axis=0)
        ys.append(g2s[u][n:] + _dot(a_r, uv2))
    for u in ids:
        uv_t = jnp.concatenate([us[u], col_of("v", u)], axis=0).T.astype(BF16)
        bk = jnp.concatenate([col_of("b_h", u), col_of("k_h", u)], axis=0).astype(BF16)
        bi, j = units[u]
        st_ref[u] = (st_ref[u] * p_end[bi][:, j * PAIR:(j + 1) * PAIR]
                     + jnp.where(blk, _dot(uv_t, bk), 0.0))

    y_all = jnp.concatenate(
        [jnp.concatenate(ys[bi * n_pairs:(bi + 1) * n_pairs], axis=1) for bi in range(nb)], axis=0)
    if reverse:
        y_ref[...] = y_all.reshape(nb, n, width)
    else:
        ysum = y_all + yb_ref[...].reshape(nb * n, width)
        inv = 1.0 / HEAD
        (mean,) = _segsum_heads([ysum], bd)
        dlt = ysum - mean * inv
        (var,) = _segsum_heads([dlt * dlt], bd)
        yn = dlt * lax.rsqrt(var * inv + GN_EPS) * lw_ref[...] + lb_ref[...]
        g = _dot(_sigmoid(cols(o + 2 * LANES, o + 3 * LANES)).astype(BF16), g2_ref[...])
        y_ref[...] = ((yn + rk_sum * v) * g).astype(y_ref.dtype).reshape(nb, n, width)


def _rwkv_scan(reverse, xs, consts, nb, yb=None):
    b, t, ncol = xs.shape
    width = consts["width"]
    n = RWKV_CHUNK
    n_chunks = t // n

    def cidx(j):
        return (n_chunks - 1 - j) if reverse else j

    def row(x):
        return pl.BlockSpec((1, x.shape[1]), lambda i, j: (0, 0))

    def full(x):
        return pl.BlockSpec(x.shape, lambda i, j: (0,) * x.ndim)

    d = 1 if reverse else 0
    args = [xs, consts["w0"][d], consts["w2"][d], consts["a0"][d], consts["a2"][d], consts["k_k"],
            consts["k_a"], consts["bd"]]
    specs = [
        pl.BlockSpec((nb, n, ncol), lambda i, j: (i, cidx(j), 0)),
        row(consts["w0"][d]), full(consts["w2"][d]), row(consts["a0"][d]),
        full(consts["a2"][d]), row(consts["k_k"]), row(consts["k_a"]), full(consts["bd"]),
    ]
    if not reverse:
        args += [consts["a0"][1], consts["a2"][1], consts["g2"], consts["r_k"], consts["lnx_w"],
                 consts["lnx_b"], yb]
        specs += [row(consts["a0"][1]), full(consts["a2"][1]), full(consts["g2"]), row(consts["r_k"]),
                  row(consts["lnx_w"]), row(consts["lnx_b"]),
                  pl.BlockSpec((nb, n, width), lambda i, j: (i, cidx(j), 0))]
    out_dtype = F32 if reverse else BF16
    return pl.pallas_call(
        functools.partial(_rwkv_kernel, reverse, nb, width),
        grid=(b // nb, n_chunks),
        in_specs=specs,
        out_specs=pl.BlockSpec((nb, n, width), lambda i, j: (i, cidx(j), 0)),
        out_shape=jax.ShapeDtypeStruct((b, t, width), out_dtype),
        scratch_shapes=[pltpu.VMEM((nb * (width // PAIR), PAIR, PAIR), F32)],
        compiler_params=_cparams(("arbitrary", "arbitrary")),
        name="rwkv_bwd" if reverse else "rwkv_fwd",
    )(*args)


def _ssd_kernel(reverse, nb, width, n_state, *refs):
    if reverse:
        (p_ref, dtb_ref, alog_ref, ed_ref, y_ref, st_ref) = refs
    else:
        (p_ref, dtb_ref, alog_ref, ed_ref, dsk_ref, nw_ref, yb_ref, y_ref, st_ref) = refs
    q = SSD_CHUNK
    n_groups = 2
    conv_ch = width + 2 * n_groups * n_state
    c1 = width + conv_ch
    heads = width // HEAD
    n_pairs = heads // 2
    pairs_per_group = n_pairs // n_groups
    d = 1 if reverse else 0

    @pl.when(pl.program_id(1) == 0)
    def _():
        st_ref[...] = jnp.zeros_like(st_ref)

    tri3 = _make_tri3(q, reverse)
    ed = ed_ref[...]
    neg_a = -jnp.exp(alog_ref[...])

    small = []
    for bi in range(nb):
        z_dt = p_ref[bi, :, c1:c1 + LANES] + dtb_ref[...]
        dt = jnp.maximum(z_dt, 0.0) + jnp.log(1.0 + jnp.exp(-jnp.abs(z_dt)))
        acs, atot = _cumsum_rows(dt * neg_a, tri3, q)
        small.append((dt, acs, jnp.broadcast_to(atot, (HALO_ROWS, LANES))))
    rows_bi = 2 * q + HALO_ROWS
    terms = _split3(jnp.concatenate([x for s in small for x in s], axis=0))
    ex = _dot(jnp.concatenate(terms, axis=0), ed)
    n_rows = nb * rows_bi
    ex = ex[:n_rows] + ex[n_rows:2 * n_rows] + ex[2 * n_rows:]

    pre = []
    for bi in range(nb):
        base = bi * rows_bi
        dt_x, acs_x, atot_x = ex[base:base + q], ex[base + q:base + 2 * q], ex[base + 2 * q:base + 2 * q + 1]
        xs = p_ref[bi, :, width:2 * width]
        xd = xs * dt_x
        acs = small[bi][1]
        pre.append(dict(acs=acs, acs_t=acs.T, xs=xs, xd=xd, e_acs=jnp.exp(acs_x),
                        xdec=xd * jnp.exp(atot_x - acs_x), p_end=jnp.exp(atot_x)))

    t_idx = lax.broadcasted_iota(jnp.int32, (q, q), 0)
    s_idx = lax.broadcasted_iota(jnp.int32, (q, q), 1)
    incl = (s_idx >= t_idx) if reverse else (s_idx <= t_idx)
    lane = lax.broadcasted_iota(jnp.int32, (1, PAIR), 1)
    m0 = lane < HEAD

    groups = [(bi, g) for bi in range(nb) for g in range(n_groups)]
    cms, bmts, cbs = [], [], []
    for bi, g in groups:
        b0 = 2 * width + g * n_state
        c0 = 2 * width + (n_groups + g) * n_state
        bm = p_ref[bi, :, b0:b0 + n_state]
        cm_b = p_ref[bi, :, c0:c0 + n_state].astype(BF16)
        cms.append(cm_b)
        cbs.append(_dot_nt(cm_b, bm.astype(BF16)))
        bmts.append(bm.T.astype(BF16))
    units = [(bi, j) for bi in range(nb) for j in range(n_pairs)]
    ys = []
    for u, (bi, j) in enumerate(units):
        pb = pre[bi]
        gi = bi * n_groups + j // pairs_per_group
        sl = slice(j * PAIR, (j + 1) * PAIR)
        ms = []
        for hh in range(2):
            li = d * heads + 2 * j + hh
            diff = pb["acs"][:, li:li + 1] - pb["acs_t"][li:li + 1, :]
            ms.append((cbs[gi] * jnp.exp(jnp.where(incl, diff, NEG_BIG))).astype(BF16))
        y_diag = _dot(jnp.concatenate(ms, axis=1), _stack2(pb["xd"][:, sl], m0).astype(BF16))
        y_off = _dot(cms[gi], st_ref[u].astype(BF16)) * pb["e_acs"][:, sl]
        ys.append(y_diag + y_off)
    for u, (bi, j) in enumerate(units):
        pb = pre[bi]
        gi = bi * n_groups + j // pairs_per_group
        sl = slice(j * PAIR, (j + 1) * PAIR)
        st_ref[u] = st_ref[u] * pb["p_end"][:, sl] + _dot(bmts[gi], pb["xdec"][:, sl].astype(BF16))

    for bi in range(nb):
        y_all = jnp.concatenate(ys[bi * n_pairs:(bi + 1) * n_pairs], axis=1)
        gate = _silu(p_ref[bi, :, 0:width])
        if reverse:
            y_ref[bi] = y_all * gate
        else:
            y = (y_all + pre[bi]["xs"] * dsk_ref[...]) * gate + yb_ref[bi]
            ms = jnp.mean(y * y, axis=-1, keepdims=True)
            y_ref[bi] = (y * lax.rsqrt(ms + GATED_EPS) * nw_ref[...]).astype(y_ref.dtype)


def _ssd_scan(reverse, p_ssm, consts, nb, yb=None):
    b, t, ncol = p_ssm.shape
    width = consts["width"]
    n_state = consts["n_state"]
    q = SSD_CHUNK
    n_chunks = t // q

    def cidx(j):
        return (n_chunks - 1 - j) if reverse else j

    def row(x):
        return pl.BlockSpec((1, x.shape[1]), lambda i, j: (0, 0))

    def full(x):
        return pl.BlockSpec(x.shape, lambda i, j: (0,) * x.ndim)

    expand = consts["expand"][1 if reverse else 0]
    args = [p_ssm, consts["dt_bias"], consts["a_log"], expand]
    specs = [
        pl.BlockSpec((nb, q, ncol), lambda i, j: (i, cidx(j), 0)),
        row(consts["dt_bias"]), row(consts["a_log"]), full(expand),
    ]
    if not reverse:
        args += [consts["d_skip"], consts["norm_w"], yb]
        specs += [row(consts["d_skip"]), row(consts["norm_w"]),
                  pl.BlockSpec((nb, q, width), lambda i, j: (i, cidx(j), 0))]
    out_dtype = F32 if reverse else BF16
    return pl.pallas_call(
        functools.partial(_ssd_kernel, reverse, nb, width, n_state),
        grid=(b // nb, n_chunks),
        in_specs=specs,
        out_specs=pl.BlockSpec((nb, q, width), lambda i, j: (i, cidx(j), 0)),
        out_shape=jax.ShapeDtypeStruct((b, t, width), out_dtype),
        scratch_shapes=[pltpu.VMEM((nb * (width // PAIR), n_state, PAIR), F32)],
        compiler_params=_cparams(("arbitrary", "arbitrary")),
        name="ssd_bwd" if reverse else "ssd_fwd",
    )(*args)


def _outproj_kernel(half, x_ref, yr_ref, ys_ref, mod_ref, nw_ref, w_ref, x1_ref, h2_ref):
    attn = _dot(yr_ref[0], w_ref[:half, :]) + _dot(ys_ref[0], w_ref[half:, :])
    x1 = x_ref[0] + mod_ref[0, 2:3, :] * attn
    x1_ref[0] = x1
    h2_ref[0] = _rms_mod(x1, nw_ref[...], mod_ref[0, 4:5, :], mod_ref[0, 3:4, :]).astype(BF16)


def _out_proj(x, y_rw, y_ssm, mod, norm_w, w_out, tm):
    b, t, d = x.shape
    half = y_rw.shape[2]
    return pl.pallas_call(
        functools.partial(_outproj_kernel, half),
        grid=(b, t // tm),
        in_specs=[
            pl.BlockSpec((1, tm, d), lambda i, j: (i, j, 0)),
            pl.BlockSpec((1, tm, half), lambda i, j: (i, j, 0)),
            pl.BlockSpec((1, tm, half), lambda i, j: (i, j, 0)),
            pl.BlockSpec((1, 6, d), lambda i, j: (i, 0, 0)),
            pl.BlockSpec((1, d), lambda i, j: (0, 0)),
            pl.BlockSpec(memory_space=pltpu.VMEM),
        ],
        out_specs=[
            pl.BlockSpec((1, tm, d), lambda i, j: (i, j, 0)),
            pl.BlockSpec((1, tm, d), lambda i, j: (i, j, 0)),
        ],
        out_shape=[
            jax.ShapeDtypeStruct((b, t, d), F32),
            jax.ShapeDtypeStruct((b, t, d), BF16),
        ],
        compiler_params=_cparams(("arbitrary", "arbitrary")),
        name="out_proj",
    )(x, y_rw, y_ssm, mod, norm_w.reshape(1, d), w_out)


def _ffn_kernel(tm, n_tiles, n_ff, h_ref, hp_ref, hn_ref, x1_hbm, mod_ref, modf_ref, fw_ref,
                wv_ref, wg_ref, cwv_ref, cwg_ref, cbv_ref, cbg_ref, wd_ref,
                o_ref, lhs_ref, uv_ref, ug_ref, x1_ref, x1_sem):
    i = pl.program_id(1)
    f = pl.program_id(2)
    hal = SUBLANES_BF16
    x1_copy = pltpu.make_async_copy(
        x1_hbm.at[pl.program_id(0), pl.ds(i * tm, tm), :], x1_ref, x1_sem)

    @pl.when(f == 0)
    def _():
        x1_copy.start()
        lhs_ref[0:hal, :] = jnp.where(i > 0, hp_ref[0], jnp.zeros_like(hp_ref[0]))
        lhs_ref[hal:hal + tm, :] = h_ref[0]
        lhs_ref[hal + tm:, :] = jnp.where(i < n_tiles - 1, hn_ref[0], jnp.zeros_like(hn_ref[0]))
        o_ref[0] = jnp.zeros(o_ref.shape[1:], F32)

    lhs = lhs_ref[...]
    ck = wv_ref.shape[1]
    subs = [(c, c + FFN_SUB) for c in range(0, ck, FFN_SUB)]
    for s, (c0, c1) in enumerate(subs):
        uv_ref[s] = _dot(lhs, wv_ref[:, c0:c1])
        ug_ref[s] = _dot(lhs, wg_ref[:, c0:c1])

    def conv(u_ref, s, cw_ref, cb_ref, c0, c1):
        out = cb_ref[:, c0:c1] + u_ref[s, pl.ds(hal - 1, tm), :] * cw_ref[0:1, c0:c1]
        out = out + u_ref[s, pl.ds(hal, tm), :] * cw_ref[1:2, c0:c1]
        return out + u_ref[s, pl.ds(hal + 1, tm), :] * cw_ref[2:3, c0:c1]

    for s, (c0, c1) in enumerate(subs):
        act = _silu(conv(uv_ref, s, cwv_ref, cbv_ref, c0, c1)) * conv(ug_ref, s, cwg_ref, cbg_ref, c0, c1)
        o_ref[0] += _dot(act.astype(BF16), wd_ref[c0:c1, :])

    @pl.when(f == n_ff - 1)
    def _():
        x1_copy.wait()
        x2 = x1_ref[...] + mod_ref[0, 5:6, :] * o_ref[0]
        o_ref[0] = _rms_mod(x2, fw_ref[...], modf_ref[0, 1:2, :], modf_ref[0, 0:1, :])


def _ffn(h2, x1, mod, modf, final_w, w_up, conv_w, conv_b, w_down, tm, ck):
    b, t, d = x1.shape
    d_ff = w_down.shape[0]
    n_ff = d_ff // ck
    n_tiles = t // tm
    hal = SUBLANES_BF16
    hb = tm // hal
    last_hb = t // hal - 1
    cb2 = conv_b.reshape(1, 2 * d_ff)
    return pl.pallas_call(
        functools.partial(_ffn_kernel, tm, n_tiles, n_ff),
        grid=(b, n_tiles, n_ff),
        in_specs=[
            pl.BlockSpec((1, tm, d), lambda i, j, f: (i, j, 0)),
            pl.BlockSpec((1, hal, d), lambda i, j, f: (i, jnp.maximum(j * hb - 1, 0), 0)),
            pl.BlockSpec((1, hal, d), lambda i, j, f: (i, jnp.minimum((j + 1) * hb, last_hb), 0)),
            pl.BlockSpec(memory_space=pl.ANY),
            pl.BlockSpec((1, 6, d), lambda i, j, f: (i, 0, 0)),
            pl.BlockSpec((1, 2, d), lambda i, j, f: (i, 0, 0)),
            pl.BlockSpec((1, d), lambda i, j, f: (0, 0)),
            pl.BlockSpec((d, ck), lambda i, j, f: (0, f)),
            pl.BlockSpec((d, ck), lambda i, j, f: (0, f + n_ff)),
            pl.BlockSpec((conv_w.shape[0], ck), lambda i, j, f: (0, f)),
            pl.BlockSpec((conv_w.shape[0], ck), lambda i, j, f: (0, f + n_ff)),
            pl.BlockSpec((1, ck), lambda i, j, f: (0, f)),
            pl.BlockSpec((1, ck), lambda i, j, f: (0, f + n_ff)),
            pl.BlockSpec((ck, d), lambda i, j, f: (f, 0)),
        ],
        out_specs=pl.BlockSpec((1, tm, d), lambda i, j, f: (i, j, 0)),
        out_shape=jax.ShapeDtypeStruct((b, t, d), F32),
        scratch_shapes=[
            pltpu.VMEM((tm + 2 * hal, d), BF16),
            pltpu.VMEM((ck // FFN_SUB, tm + 2 * hal, FFN_SUB), F32),
            pltpu.VMEM((ck // FFN_SUB, tm + 2 * hal, FFN_SUB), F32),
            pltpu.VMEM((tm, d), F32),
            pltpu.SemaphoreType.DMA(()),
        ],
        compiler_params=_cparams(("arbitrary", "arbitrary", "arbitrary")),
        name="conv_glu_ffn",
    )(h2, h2, h2, x1, mod, modf, final_w.reshape(1, d), w_up, w_up, conv_w, conv_w, cb2, cb2, w_down)


def _pad_rows(x, rows):
    return jnp.concatenate([x, jnp.zeros((rows - x.shape[0],) + x.shape[1:], x.dtype)], axis=0)


def _pad_lanes(x, lanes):
    return jnp.concatenate([x, jnp.zeros(x.shape[:-1] + (lanes - x.shape[-1],), x.dtype)], axis=-1)


def _encoder(x, mod, modf, wts):
    rw, ssm = wts["rwkv"], wts["ssm"]
    xs_rw, p_ssm = _in_proj(x, mod, wts["norm1_w"], wts["w_in"], rw["mu"], ssm["conv_w"],
                            ssm["conv_b"], rw["ncol"], rw["width"], tm=256)
    nb = math.gcd(x.shape[0], SCAN_ROWS)
    yb = _rwkv_scan(True, xs_rw, rw, nb)
    y_rw = _rwkv_scan(False, xs_rw, rw, nb, yb)
    sb = _ssd_scan(True, p_ssm, ssm, nb)
    y_ssm = _ssd_scan(False, p_ssm, ssm, nb, sb)
    x1, h2 = _out_proj(x, y_rw, y_ssm, mod, wts["norm2_w"], wts["w_out"], tm=512)
    return _ffn(h2, x1, mod, modf, wts["final_norm_w"], wts["w_up"], wts["ffn_conv_w"],
                wts["ffn_conv_b"], wts["w_down"], tm=512, ck=512)


def kernel(x_prompt, x_sample, c_prompt, c_sample, norm1_w, w_in, rwkv_mu, rwkv_w0, rwkv_w2, rwkv_a0, rwkv_a2, rwkv_g2, rwkv_k_k, rwkv_k_a, rwkv_r_k, rwkv_lnx_w, rwkv_lnx_b, ssm_conv_w, ssm_conv_b, ssm_dt_bias, ssm_a_log, ssm_d, ssm_norm_w, w_out, norm2_w, ffn_w_up, ffn_conv_w, ffn_conv_b, ffn_w_down, w_ada, b_ada, final_norm_w, w_ada_final, b_ada_final):
    assert w_in.shape[0] == 1, "single-layer trunk"
    d = x_prompt.shape[-1]
    width = rwkv_w0.shape[-1]
    heads = width // HEAD
    lora = rwkv_w2.shape[2]
    n_rw = rwkv_mu.shape[-1]
    n_state = (ssm_conv_w.shape[-1] - width) // 4
    assert 2 * lora == LANES and rwkv_g2.shape[1] == LANES and 2 * heads <= LANES

    n_ssm = w_in.shape[2] - n_rw
    n_ssm_pad = -(-n_ssm // LANES) * LANES
    w_cat = _pad_lanes(w_in[0], n_rw + n_ssm_pad).astype(BF16)
    zl = jnp.zeros((lora, width), F32)
    eye_h = jnp.repeat(jnp.eye(heads, dtype=F32), HEAD, axis=1)
    seg = jnp.arange(MXU_DIM) // HEAD
    rwkv = {
        "width": width, "ncol": n_rw,
        "mu": rwkv_mu[0],
        "w0": [rwkv_w0[0, i:i + 1] for i in range(2)],
        "a0": [rwkv_a0[0, i:i + 1] for i in range(2)],
        "w2": [jnp.concatenate([rwkv_w2[0, 0], zl], 0).astype(BF16),
               jnp.concatenate([zl, rwkv_w2[0, 1]], 0).astype(BF16)],
        "a2": [jnp.concatenate([rwkv_a2[0, 0], zl], 0).astype(BF16),
               jnp.concatenate([zl, rwkv_a2[0, 1]], 0).astype(BF16)],
        "g2": rwkv_g2[0].astype(BF16),
        "k_k": rwkv_k_k, "k_a": rwkv_k_a, "r_k": rwkv_r_k[0].reshape(1, width),
        "lnx_w": rwkv_lnx_w, "lnx_b": rwkv_lnx_b,
        "bd": (seg[:, None] == seg[None, :]).astype(BF16),
    }
    ssm = {
        "width": width, "n_state": n_state,
        "conv_w": ssm_conv_w[0], "conv_b": ssm_conv_b,
        "dt_bias": _pad_lanes(ssm_dt_bias[0].reshape(1, 2 * heads), LANES),
        "a_log": _pad_lanes(ssm_a_log[0].reshape(1, 2 * heads), LANES),
        "d_skip": jnp.repeat(ssm_d[0], HEAD).reshape(1, width),
        "norm_w": ssm_norm_w,
        "expand": [_pad_rows(jnp.concatenate([eye_h * (1 - i), eye_h * i], 0), LANES).astype(BF16)
                   for i in range(2)],
    }
    wts = {
        "norm1_w": norm1_w[0], "norm2_w": norm2_w[0], "final_norm_w": final_norm_w,
        "w_in": w_cat, "w_out": w_out[0].astype(BF16),
        "w_up": ffn_w_up[0].astype(BF16), "w_down": ffn_w_down[0].astype(BF16),
        "ffn_conv_w": ffn_conv_w[0], "ffn_conv_b": ffn_conv_b[0],
        "rwkv": rwkv, "ssm": ssm,
    }

    nb_p, nb_s = c_prompt.shape[0], c_sample.shape[0]
    rows = -(-(nb_p + nb_s) // 8) * 8
    c_all = _pad_rows(jnp.concatenate([c_prompt, c_sample], 0), rows)
    mod_all = _ada(c_all, w_ada[0], b_ada[0]).reshape(rows, 6, d)
    modf_all = _ada(c_all, w_ada_final, b_ada_final).reshape(rows, 2, d)

    outs = []
    for x, lo, hi in ((x_prompt, 0, nb_p), (x_sample, nb_p, nb_p + nb_s)):
        outs.append(_encoder(x, mod_all[lo:hi], modf_all[lo:hi], wts))
    return tuple(outs)
```

```python
import functools
import math

import jax
import jax.numpy as jnp
from jax import lax
from jax.experimental import pallas as pl
from jax.experimental.pallas import tpu as pltpu

F32 = jnp.float32
BF16 = jnp.bfloat16

LANES = 128
SUBLANES_BF16 = 16
MXU_DIM = 256
VMEM_LIMIT_BYTES = 60 * 1024 * 1024

NORM_EPS = 1e-6
GN_EPS = 64e-5
GATED_EPS = 1e-5

HEAD = 64
PAIR = 2 * HEAD
RWKV_CHUNK = 64
SSD_CHUNK = 128
SCAN_ROWS = 4
HALO_ROWS = 8
INPROJ_BLOCK = MXU_DIM
FFN_SUB = MXU_DIM
NEG_BIG = -1e30


def _cparams(sem):
    return pltpu.CompilerParams(dimension_semantics=sem, vmem_limit_bytes=VMEM_LIMIT_BYTES)


def _dot(a, b):
    return jnp.dot(a, b, preferred_element_type=F32)


def _dot_nt(a, b):
    return lax.dot_general(a, b, (((1,), (1,)), ((), ())), preferred_element_type=F32)


def _sigmoid(x):
    return 1.0 / (1.0 + jnp.exp(-x))


def _silu(x):
    return x * _sigmoid(x)


def _split3(x):
    hi = x.astype(BF16)
    r1 = x - hi.astype(F32)
    mid = r1.astype(BF16)
    lo = (r1 - mid.astype(F32)).astype(BF16)
    return hi, mid, lo


def _cumsum_rows(x, tri3, n):
    hi, mid, lo = _split3(x)
    stacked = jnp.concatenate([hi, mid, lo], axis=0)
    out = _dot(tri3, stacked)
    return out[:n], out[n:n + 1]


def _make_tri3(n, reverse):
    rows = lax.broadcasted_iota(jnp.int32, (n + 8, 3 * n), 0)
    cols = lax.broadcasted_iota(jnp.int32, (n + 8, 3 * n), 1) & (n - 1)
    tri = (cols >= rows) if reverse else (cols <= rows)
    return jnp.where(tri | (rows >= n), 1.0, 0.0).astype(BF16)


def _segsum_heads(xs, bd):
    n = xs[0].shape[0]
    parts = []
    for x in xs:
        hi = x.astype(BF16)
        parts += [hi, (x - hi.astype(F32)).astype(BF16)]
    stacked = jnp.concatenate(parts, axis=0)
    out = jnp.concatenate(
        [_dot(stacked[:, i * MXU_DIM:(i + 1) * MXU_DIM], bd)
         for i in range(stacked.shape[1] // MXU_DIM)], axis=1)
    return [out[2 * i * n:(2 * i + 1) * n] + out[(2 * i + 1) * n:(2 * i + 2) * n]
            for i in range(len(xs))]


def _rms_mod(x, w, scale, shift):
    ms = jnp.mean(x * x, axis=-1, keepdims=True)
    return x * lax.rsqrt(ms + NORM_EPS) * w * (1.0 + scale) + shift


def _ada_kernel(c_ref, w_ref, b_ref, o_ref):
    c = c_ref[...]
    o_ref[...] = _dot(_silu(c), w_ref[...]) + b_ref[...]


def _ada(c_pad, w, b):
    rows, d = c_pad.shape
    n = w.shape[1]
    tn = 1024
    return pl.pallas_call(
        _ada_kernel,
        grid=(n // tn,),
        in_specs=[
            pl.BlockSpec((rows, d), lambda j: (0, 0)),
            pl.BlockSpec((d, tn), lambda j: (0, j)),
            pl.BlockSpec((1, tn), lambda j: (0, j)),
        ],
        out_specs=pl.BlockSpec((rows, tn), lambda j: (0, j)),
        out_shape=jax.ShapeDtypeStruct((rows, n), F32),
        compiler_params=_cparams(("arbitrary",)),
        name="ada_mod",
    )(c_pad, w, b.reshape(1, n))


def _inproj_kernel(tm, n_tiles, n_rw, width, x_ref, xp_ref, xn_ref, mod_ref, nw_ref, w_ref, mu_ref,
                   cw_ref, cb_ref, orw_ref, ossm_ref):
    j = pl.program_id(1)
    nw, scale, shift = nw_ref[...], mod_ref[0, 1:2, :], mod_ref[0, 0:1, :]
    h_prev = jnp.where(j > 0, _rms_mod(xp_ref[0], nw, scale, shift), 0.0)
    h_next = jnp.where(j < n_tiles - 1, _rms_mod(xn_ref[0], nw, scale, shift), 0.0)
    h = jnp.concatenate([h_prev, _rms_mod(x_ref[0], nw, scale, shift), h_next], axis=0).astype(BF16)
    rows = tm + 2 * HALO_ROWS
    mid = slice(HALO_ROWS, HALO_ROWS + tm)

    conv_k, conv_ch = cw_ref.shape
    left = (conv_k - 1) // 2
    n_ssm = ossm_ref.shape[2]

    def blocks(lo, hi):
        return [(c, min(c + INPROJ_BLOCK, hi)) for c in range(lo, hi, INPROJ_BLOCK)]

    def shift_tail(p, c0, c1):
        p_prev = pltpu.roll(p, 1, 0)
        p_next = pltpu.roll(p, rows - 1, 0)
        xs = p + mu_ref[0:1, c0:c1] * (p_prev - p) + mu_ref[1:2, c0:c1] * (p_next - p)
        orw_ref[0, :, c0:c1] = xs[mid]

    def copy_tail(p, c0, c1):
        ossm_ref[0, :, c0:c1] = p[mid]

    def conv_tail(ext, c0, c1):
        acc = cb_ref[:, c0:c1] + ext[mid] * cw_ref[left:left + 1, c0:c1]
        for i in range(conv_k):
            if i != left:
                shifted = pltpu.roll(ext, (left - i) % rows, 0)
                acc = acc + shifted[mid] * cw_ref[i:i + 1, c0:c1]
        ossm_ref[0, :, width + c0:width + c1] = _silu(acc)

    work = [(shift_tail, 0, c0, c1) for c0, c1 in blocks(0, n_rw)]
    work += [(conv_tail, n_rw + width, c0, c1) for c0, c1 in blocks(0, conv_ch)]
    work += [(copy_tail, n_rw, c0, c1) for c0, c1 in blocks(0, width) + blocks(width + conv_ch, n_ssm)]
    for tail, base, c0, c1 in work:
        tail(_dot(h, w_ref[:, base + c0:base + c1]), c0, c1)


def _in_proj(x, mod, norm_w, w_cat, mu, conv_w, conv_b, n_rw, width, tm):
    b, t, d = x.shape
    n_ssm = w_cat.shape[1] - n_rw
    n_tiles = t // tm
    hb = tm // HALO_ROWS
    last_hb = t // HALO_ROWS - 1
    return pl.pallas_call(
        functools.partial(_inproj_kernel, tm, n_tiles, n_rw, width),
        grid=(b, n_tiles),
        in_specs=[
            pl.BlockSpec((1, tm, d), lambda i, j: (i, j, 0)),
            pl.BlockSpec((1, HALO_ROWS, d), lambda i, j: (i, jnp.maximum(j * hb - 1, 0), 0)),
            pl.BlockSpec((1, HALO_ROWS, d), lambda i, j: (i, jnp.minimum((j + 1) * hb, last_hb), 0)),
            pl.BlockSpec((1, 6, d), lambda i, j: (i, 0, 0)),
            pl.BlockSpec((1, d), lambda i, j: (0, 0)),
            pl.BlockSpec(memory_space=pltpu.VMEM),
            pl.BlockSpec(mu.shape, lambda i, j: (0, 0)),
            pl.BlockSpec(conv_w.shape, lambda i, j: (0, 0)),
            pl.BlockSpec(conv_b.shape, lambda i, j: (0, 0)),
        ],
        out_specs=[
            pl.BlockSpec((1, tm, n_rw), lambda i, j: (i, j, 0)),
            pl.BlockSpec((1, tm, n_ssm), lambda i, j: (i, j, 0)),
        ],
        out_shape=[
            jax.ShapeDtypeStruct((b, t, n_rw), F32),
            jax.ShapeDtypeStruct((b, t, n_ssm), F32),
        ],
        compiler_params=_cparams(("arbitrary", "arbitrary")),
        name="in_proj",
    )(x, x, x, mod, norm_w.reshape(1, d), w_cat, mu, conv_w, conv_b)


def _stack2(x, m0):
    return jnp.concatenate([jnp.where(m0, x, 0.0), jnp.where(m0, 0.0, x)], axis=0)


def _rwkv_kernel(reverse, nb, width, *refs):
    if reverse:
        (xs_ref, w0_ref, w2_ref, a0_ref, a2_ref, kk_ref, ka_ref, bd_ref, y_ref, st_ref) = refs
    else:
        (xs_ref, w0_ref, w2_ref, a0_ref, a2_ref, kk_ref, ka_ref, bd_ref,
         a0o_ref, a2o_ref, g2_ref, rk_ref, lw_ref, lb_ref, yb_ref, y_ref, st_ref) = refs
    n = RWKV_CHUNK
    n_pairs = width // PAIR

    @pl.when(pl.program_id(1) == 0)
    def _():
        st_ref[...] = jnp.zeros_like(st_ref)

    bd = bd_ref[...]
    k_a = ka_ref[...]
    tri3 = _make_tri3(n, reverse)
    o = 3 * width

    def cols(c0, c1):
        return xs_ref[:, :, c0:c1].reshape(nb * n, c1 - c0)

    r, k, v = cols(0, width), cols(width, 2 * width), cols(2 * width, o)
    ad_b = cols(o + LANES, o + 2 * LANES).astype(BF16)
    w_log = w0_ref[...] + _dot(jnp.tanh(cols(o, o + LANES)).astype(BF16), w2_ref[...])
    lw = -math.exp(-0.5) * _sigmoid(w_log)
    a = _sigmoid(a0_ref[...] + _dot(ad_b, a2_ref[...]))
    kk = k * kk_ref[...]
    k_d = k * (1.0 + (a - 1.0) * k_a)
    if reverse:
        (kk_sq,) = _segsum_heads([kk * kk], bd)
    else:
        a_o = _sigmoid(a0o_ref[...] + _dot(ad_b, a2o_ref[...]))
        ksum = k_d + k * (1.0 + (a_o - 1.0) * k_a)
        kk_sq, rk_sum = _segsum_heads([kk * kk, r * ksum * rk_ref[...]], bd)
    kk = kk / jnp.maximum(jnp.sqrt(kk_sq), 1e-12)
    beta = kk * a
    cums = [_cumsum_rows(lw[bi * n:(bi + 1) * n], tri3, n) for bi in range(nb)]
    c = jnp.concatenate([cu[0] for cu in cums], axis=0)
    ctot_rows = jnp.concatenate([jnp.broadcast_to(cu[1], (n, width)) for cu in cums], axis=0)
    e_nc = jnp.exp(-c)
    e_end = jnp.exp(ctot_rows - c)
    pre = dict(v=v, a_t=-kk * jnp.exp(c - lw), r_t=r * jnp.exp(c), b_t=beta * e_nc, k_t=k_d * e_nc,
               b_h=beta * e_end, k_h=k_d * e_end)
    p_end = [jnp.exp(cu[1]) for cu in cums]

    lane = lax.broadcasted_iota(jnp.int32, (1, PAIR), 1)
    m0 = lane < HEAD
    t_idx = lax.broadcasted_iota(jnp.int32, (n, 2 * n), 0)
    col = lax.broadcasted_iota(jnp.int32, (n, 2 * n), 1)
    s_idx = col & (n - 1)
    col_h0 = col < n
    if reverse:
        strict, incl = s_idx > t_idx, s_idx >= t_idx
    else:
        strict, incl = s_idx < t_idx, s_idx <= t_idx
    incl2 = jnp.concatenate([incl, incl], axis=1)
    eye2 = jnp.where(s_idx == t_idx, 1.0, 0.0)
    br = lax.broadcasted_iota(jnp.int32, (PAIR, PAIR), 0) < HEAD
    bc = lax.broadcasted_iota(jnp.int32, (PAIR, PAIR), 1) < HEAD
    blk = br == bc

    def blockdiag(x):
        return jnp.concatenate(
            [jnp.where(col_h0, x, 0.0), jnp.where(col_h0, 0.0, x)], axis=0).astype(BF16)

    units = [(bi, j) for bi in range(nb) for j in range(n_pairs)]
    ids = range(len(units))

    def col_of(name, u):
        bi, j = units[u]
        return pre[name][bi * n:(bi + 1) * n, j * PAIR:(j + 1) * PAIR]

    g1s, g2s = [], []
    for u in ids:
        lhs = jnp.concatenate([col_of("a_t", u), col_of("r_t", u)], axis=0).astype(BF16)
        rhs = jnp.concatenate([_stack2(col_of("b_t", u), m0), _stack2(col_of("k_t", u), m0)],
                              axis=0).astype(BF16)
        g1s.append(_dot_nt(lhs, rhs))
        g2s.append(_dot_nt(lhs, st_ref[u].astype(BF16)))
    a_ab = [jnp.where(strict, g1s[u][:n, :2 * n], 0.0) for u in ids]
    n_fac = int(math.log2(n))
    tmat = [eye2 + a_ab[u] for u in ids]
    pw = [_dot(a_ab[u].astype(BF16), blockdiag(a_ab[u])) for u in ids]
    for _ in range(1, n_fac - 1):
        both = [_dot(jnp.concatenate([tmat[u], pw[u]], axis=0).astype(BF16), blockdiag(pw[u]))
                for u in ids]
        tmat = [tmat[u] + both[u][:n] for u in ids]
        pw = [both[u][n:] for u in ids]
    tmat = [tmat[u] + _dot(tmat[u].astype(BF16), blockdiag(pw[u])) for u in ids]
    v2 = [_stack2(col_of("v", u), m0).astype(BF16) for u in ids]
    w_rhs = [g2s[u][:n] + _dot(jnp.where(strict, g1s[u][:n, 2 * n:], 0.0).astype(BF16), v2[u])
             for u in ids]
    us = [_dot(tmat[u].astype(BF16), _stack2(w_rhs[u], m0).astype(BF16)) for u in ids]
    ys = []
    for u in ids:
        a_r = jnp.where(incl2, g1s[u][n:], 0.0).astype(BF16)
        uv2 = jnp.concatenate([_stack2(us[u], m0).astype(BF16), v2[u]], axis=0)
        ys.append(g2s[u][n:] + _dot(a_r, uv2))
    for u in ids:
        uv_t = jnp.concatenate([us[u], col_of("v", u)], axis=0).T.astype(BF16)
        bk = jnp.concatenate([col_of("b_h", u), col_of("k_h", u)], axis=0).astype(BF16)
        bi, j = units[u]
        st_ref[u] = (st_ref[u] * p_end[bi][:, j * PAIR:(j + 1) * PAIR]
                     + jnp.where(blk, _dot(uv_t, bk), 0.0))

    y_all = jnp.concatenate(
        [jnp.concatenate(ys[bi * n_pairs:(bi + 1) * n_pairs], axis=1) for bi in range(nb)], axis=0)
    if reverse:
        y_ref[...] = y_all.reshape(nb, n, width)
    else:
        ysum = y_all + yb_ref[...].reshape(nb * n, width)
        inv = 1.0 / HEAD
        (mean,) = _segsum_heads([ysum], bd)
        dlt = ysum - mean * inv
        (var,) = _segsum_heads([dlt * dlt], bd)
        yn = dlt * lax.rsqrt(var * inv + GN_EPS) * lw_ref[...] + lb_ref[...]
        g = _dot(_sigmoid(cols(o + 2 * LANES, o + 3 * LANES)).astype(BF16), g2_ref[...])
        y_ref[...] = ((yn + rk_sum * v) * g).astype(y_ref.dtype).reshape(nb, n, width)


def _rwkv_scan(reverse, xs, consts, nb, yb=None):
    b, t, ncol = xs.shape
    width = consts["width"]
    n = RWKV_CHUNK
    n_chunks = t // n

    def cidx(j):
        return (n_chunks - 1 - j) if reverse else j

    def row(x):
        return pl.BlockSpec((1, x.shape[1]), lambda i, j: (0, 0))

    def full(x):
        return pl.BlockSpec(x.shape, lambda i, j: (0,) * x.ndim)

    d = 1 if reverse else 0
    args = [xs, consts["w0"][d], consts["w2"][d], consts["a0"][d], consts["a2"][d], consts["k_k"],
            consts["k_a"], consts["bd"]]
    specs = [
        pl.BlockSpec((nb, n, ncol), lambda i, j: (i, cidx(j), 0)),
        row(consts["w0"][d]), full(consts["w2"][d]), row(consts["a0"][d]),
        full(consts["a2"][d]), row(consts["k_k"]), row(consts["k_a"]), full(consts["bd"]),
    ]
    if not reverse:
        args += [consts["a0"][1], consts["a2"][1], consts["g2"], consts["r_k"], consts["lnx_w"],
                 consts["lnx_b"], yb]
        specs += [row(consts["a0"][1]), full(consts["a2"][1]), full(consts["g2"]), row(consts["r_k"]),
                  row(consts["lnx_w"]), row(consts["lnx_b"]),
                  pl.BlockSpec((nb, n, width), lambda i, j: (i, cidx(j), 0))]
    out_dtype = F32 if reverse else BF16
    return pl.pallas_call(
        functools.partial(_rwkv_kernel, reverse, nb, width),
        grid=(b // nb, n_chunks),
        in_specs=specs,
        out_specs=pl.BlockSpec((nb, n, width), lambda i, j: (i, cidx(j), 0)),
        out_shape=jax.ShapeDtypeStruct((b, t, width), out_dtype),
        scratch_shapes=[pltpu.VMEM((nb * (width // PAIR), PAIR, PAIR), F32)],
        compiler_params=_cparams(("arbitrary", "arbitrary")),
        name="rwkv_bwd" if reverse else "rwkv_fwd",
    )(*args)


def _ssd_kernel(reverse, nb, width, n_state, *refs):
    if reverse:
        (p_ref, dtb_ref, alog_ref, ed_ref, y_ref, st_ref) = refs
    else:
        (p_ref, dtb_ref, alog_ref, ed_ref, dsk_ref, nw_ref, yb_ref, y_ref, st_ref) = refs
    q = SSD_CHUNK
    n_groups = 2
    conv_ch = width + 2 * n_groups * n_state
    c1 = width + conv_ch
    heads = width // HEAD
    n_pairs = heads // 2
    pairs_per_group = n_pairs // n_groups
    d = 1 if reverse else 0

    @pl.when(pl.program_id(1) == 0)
    def _():
        st_ref[...] = jnp.zeros_like(st_ref)

    tri3 = _make_tri3(q, reverse)
    ed = ed_ref[...]
    neg_a = -jnp.exp(alog_ref[...])

    small = []
    for bi in range(nb):
        z_dt = p_ref[bi, :, c1:c1 + LANES] + dtb_ref[...]
        dt = jnp.maximum(z_dt, 0.0) + jnp.log(1.0 + jnp.exp(-jnp.abs(z_dt)))
        acs, atot = _cumsum_rows(dt * neg_a, tri3, q)
        small.append((dt, acs, jnp.broadcast_to(atot, (HALO_ROWS, LANES))))
    rows_bi = 2 * q + HALO_ROWS
    terms = _split3(jnp.concatenate([x for s in small for x in s], axis=0))
    ex = _dot(jnp.concatenate(terms, axis=0), ed)
    n_rows = nb * rows_bi
    ex = ex[:n_rows] + ex[n_rows:2 * n_rows] + ex[2 * n_rows:]

    pre = []
    for bi in range(nb):
        base = bi * rows_bi
        dt_x, acs_x, atot_x = ex[base:base + q], ex[base + q:base + 2 * q], ex[base + 2 * q:base + 2 * q + 1]
        xs = p_ref[bi, :, width:2 * width]
        xd = xs * dt_x
        acs = small[bi][1]
        pre.append(dict(acs=acs, acs_t=acs.T, xs=xs, xd=xd, e_acs=jnp.exp(acs_x),
                        xdec=xd * jnp.exp(atot_x - acs_x), p_end=jnp.exp(atot_x)))

    t_idx = lax.broadcasted_iota(jnp.int32, (q, q), 0)
    s_idx = lax.broadcasted_iota(jnp.int32, (q, q), 1)
    incl = (s_idx >= t_idx) if reverse else (s_idx <= t_idx)
    lane = lax.broadcasted_iota(jnp.int32, (1, PAIR), 1)
    m0 = lane < HEAD

    groups = [(bi, g) for bi in range(nb) for g in range(n_groups)]
    cms, bmts, cbs = [], [], []
    for bi, g in groups:
        b0 = 2 * width + g * n_state
        c0 = 2 * width + (n_groups + g) * n_state
        bm = p_ref[bi, :, b0:b0 + n_state]
        cm_b = p_ref[bi, :, c0:c0 + n_state].astype(BF16)
        cms.append(cm_b)
        cbs.append(_dot_nt(cm_b, bm.astype(BF16)))
        bmts.append(bm.T.astype(BF16))
    units = [(bi, j) for bi in range(nb) for j in range(n_pairs)]
    ys = []
    for u, (bi, j) in enumerate(units):
        pb = pre[bi]
        gi = bi * n_groups + j // pairs_per_group
        sl = slice(j * PAIR, (j + 1) * PAIR)
        ms = []
        for hh in range(2):
            li = d * heads + 2 * j + hh
            diff = pb["acs"][:, li:li + 1] - pb["acs_t"][li:li + 1, :]
            ms.append((cbs[gi] * jnp.exp(jnp.where(incl, diff, NEG_BIG))).astype(BF16))
        y_diag = _dot(jnp.concatenate(ms, axis=1), _stack2(pb["xd"][:, sl], m0).astype(BF16))
        y_off = _dot(cms[gi], st_ref[u].astype(BF16)) * pb["e_acs"][:, sl]
        ys.append(y_diag + y_off)
    for u, (bi, j) in enumerate(units):
        pb = pre[bi]
        gi = bi * n_groups + j // pairs_per_group
        sl = slice(j * PAIR, (j + 1) * PAIR)
        st_ref[u] = st_ref[u] * pb["p_end"][:, sl] + _dot(bmts[gi], pb["xdec"][:, sl].astype(BF16))

    for bi in range(nb):
        y_all = jnp.concatenate(ys[bi * n_pairs:(bi + 1) * n_pairs], axis=1)
        gate = _silu(p_ref[bi, :, 0:width])
        if reverse:
            y_ref[bi] = y_all * gate
        else:
            y = (y_all + pre[bi]["xs"] * dsk_ref[...]) * gate + yb_ref[bi]
            ms = jnp.mean(y * y, axis=-1, keepdims=True)
            y_ref[bi] = (y * lax.rsqrt(ms + GATED_EPS) * nw_ref[...]).astype(y_ref.dtype)


def _ssd_scan(reverse, p_ssm, consts, nb, yb=None):
    b, t, ncol = p_ssm.shape
    width = consts["width"]
    n_state = consts["n_state"]
    q = SSD_CHUNK
    n_chunks = t // q

    def cidx(j):
        return (n_chunks - 1 - j) if reverse else j

    def row(x):
        return pl.BlockSpec((1, x.shape[1]), lambda i, j: (0, 0))

    def full(x):
        return pl.BlockSpec(x.shape, lambda i, j: (0,) * x.ndim)

    expand = consts["expand"][1 if reverse else 0]
    args = [p_ssm, consts["dt_bias"], consts["a_log"], expand]
    specs = [
        pl.BlockSpec((nb, q, ncol), lambda i, j: (i, cidx(j), 0)),
        row(consts["dt_bias"]), row(consts["a_log"]), full(expand),
    ]
    if not reverse:
        args += [consts["d_skip"], consts["norm_w"], yb]
        specs += [row(consts["d_skip"]), row(consts["norm_w"]),
                  pl.BlockSpec((nb, q, width), lambda i, j: (i, cidx(j), 0))]
    out_dtype = F32 if reverse else BF16
    return pl.pallas_call(
        functools.partial(_ssd_kernel, reverse, nb, width, n_state),
        grid=(b // nb, n_chunks),
        in_specs=specs,
        out_specs=pl.BlockSpec((nb, q, width), lambda i, j: (i, cidx(j), 0)),
        out_shape=jax.ShapeDtypeStruct((b, t, width), out_dtype),
        scratch_shapes=[pltpu.VMEM((nb * (width // PAIR), n_state, PAIR), F32)],
        compiler_params=_cparams(("arbitrary", "arbitrary")),
        name="ssd_bwd" if reverse else "ssd_fwd",
    )(*args)


def _outproj_kernel(half, x_ref, yr_ref, ys_ref, mod_ref, nw_ref, w_ref, x1_ref, h2_ref):
    attn = _dot(yr_ref[0], w_ref[:half, :]) + _dot(ys_ref[0], w_ref[half:, :])
    x1 = x_ref[0] + mod_ref[0, 2:3, :] * attn
    x1_ref[0] = x1
    h2_ref[0] = _rms_mod(x1, nw_ref[...], mod_ref[0, 4:5, :], mod_ref[0, 3:4, :]).astype(BF16)


def _out_proj(x, y_rw, y_ssm, mod, norm_w, w_out, tm):
    b, t, d = x.shape
    half = y_rw.shape[2]
    return pl.pallas_call(
        functools.partial(_outproj_kernel, half),
        grid=(b, t // tm),
        in_specs=[
            pl.BlockSpec((1, tm, d), lambda i, j: (i, j, 0)),
            pl.BlockSpec((1, tm, half), lambda i, j: (i, j, 0)),
            pl.BlockSpec((1, tm, half), lambda i, j: (i, j, 0)),
            pl.BlockSpec((1, 6, d), lambda i, j: (i, 0, 0)),
            pl.BlockSpec((1, d), lambda i, j: (0, 0)),
            pl.BlockSpec(memory_space=pltpu.VMEM),
        ],
        out_specs=[
            pl.BlockSpec((1, tm, d), lambda i, j: (i, j, 0)),
            pl.BlockSpec((1, tm, d), lambda i, j: (i, j, 0)),
        ],
        out_shape=[
            jax.ShapeDtypeStruct((b, t, d), F32),
            jax.ShapeDtypeStruct((b, t, d), BF16),
        ],
        compiler_params=_cparams(("arbitrary", "arbitrary")),
        name="out_proj",
    )(x, y_rw, y_ssm, mod, norm_w.reshape(1, d), w_out)


def _ffn_kernel(tm, n_tiles, n_ff, h_ref, hp_ref, hn_ref, x1_hbm, mod_ref, modf_ref, fw_ref,
                wv_ref, wg_ref, cwv_ref, cwg_ref, cbv_ref, cbg_ref, wd_ref,
                o_ref, lhs_ref, uv_ref, ug_ref, x1_ref, x1_sem):
    i = pl.program_id(1)
    f = pl.program_id(2)
    hal = SUBLANES_BF16
    x1_copy = pltpu.make_async_copy(
        x1_hbm.at[pl.program_id(0), pl.ds(i * tm, tm), :], x1_ref, x1_sem)

    @pl.when(f == 0)
    def _():
        x1_copy.start()
        lhs_ref[0:hal, :] = jnp.where(i > 0, hp_ref[0], jnp.zeros_like(hp_ref[0]))
        lhs_ref[hal:hal + tm, :] = h_ref[0]
        lhs_ref[hal + tm:, :] = jnp.where(i < n_tiles - 1, hn_ref[0], jnp.zeros_like(hn_ref[0]))
        o_ref[0] = jnp.zeros(o_ref.shape[1:], F32)

    lhs = lhs_ref[...]
    ck = wv_ref.shape[1]
    subs = [(c, c + FFN_SUB) for c in range(0, ck, FFN_SUB)]
    for s, (c0, c1) in enumerate(subs):
        uv_ref[s] = _dot(lhs, wv_ref[:, c0:c1])
        ug_ref[s] = _dot(lhs, wg_ref[:, c0:c1])

    def conv(u_ref, s, cw_ref, cb_ref, c0, c1):
        out = cb_ref[:, c0:c1] + u_ref[s, pl.ds(hal - 1, tm), :] * cw_ref[0:1, c0:c1]
        out = out + u_ref[s, pl.ds(hal, tm), :] * cw_ref[1:2, c0:c1]
        return out + u_ref[s, pl.ds(hal + 1, tm), :] * cw_ref[2:3, c0:c1]

    for s, (c0, c1) in enumerate(subs):
        act = _silu(conv(uv_ref, s, cwv_ref, cbv_ref, c0, c1)) * conv(ug_ref, s, cwg_ref, cbg_ref, c0, c1)
        o_ref[0] += _dot(act.astype(BF16), wd_ref[c0:c1, :])

    @pl.when(f == n_ff - 1)
    def _():
        x1_copy.wait()
        x2 = x1_ref[...] + mod_ref[0, 5:6, :] * o_ref[0]
        o_ref[0] = _rms_mod(x2, fw_ref[...], modf_ref[0, 1:2, :], modf_ref[0, 0:1, :])


def _ffn(h2, x1, mod, modf, final_w, w_up, conv_w, conv_b, w_down, tm, ck):
    b, t, d = x1.shape
    d_ff = w_down.shape[0]
    n_ff = d_ff // ck
    n_tiles = t // tm
    hal = SUBLANES_BF16
    hb = tm // hal
    last_hb = t // hal - 1
    cb2 = conv_b.reshape(1, 2 * d_ff)
    return pl.pallas_call(
        functools.partial(_ffn_kernel, tm, n_tiles, n_ff),
        grid=(b, n_tiles, n_ff),
        in_specs=[
            pl.BlockSpec((1, tm, d), lambda i, j, f: (i, j, 0)),
            pl.BlockSpec((1, hal, d), lambda i, j, f: (i, jnp.maximum(j * hb - 1, 0), 0)),
            pl.BlockSpec((1, hal, d), lambda i, j, f: (i, jnp.minimum((j + 1) * hb, last_hb), 0)),
            pl.BlockSpec(memory_space=pl.ANY),
            pl.BlockSpec((1, 6, d), lambda i, j, f: (i, 0, 0)),
            pl.BlockSpec((1, 2, d), lambda i, j, f: (i, 0, 0)),
            pl.BlockSpec((1, d), lambda i, j, f: (0, 0)),
            pl.BlockSpec((d, ck), lambda i, j, f: (0, f)),
            pl.BlockSpec((d, ck), lambda i, j, f: (0, f + n_ff)),
            pl.BlockSpec((conv_w.shape[0], ck), lambda i, j, f: (0, f)),
            pl.BlockSpec((conv_w.shape[0], ck), lambda i, j, f: (0, f + n_ff)),
            pl.BlockSpec((1, ck), lambda i, j, f: (0, f)),
            pl.BlockSpec((1, ck), lambda i, j, f: (0, f + n_ff)),
            pl.BlockSpec((ck, d), lambda i, j, f: (f, 0)),
        ],
        out_specs=pl.BlockSpec((1, tm, d), lambda i, j, f: (i, j, 0)),
        out_shape=jax.ShapeDtypeStruct((b, t, d), F32),
        scratch_shapes=[
            pltpu.VMEM((tm + 2 * hal, d), BF16),
            pltpu.VMEM((ck // FFN_SUB, tm + 2 * hal, FFN_SUB), F32),
            pltpu.VMEM((ck // FFN_SUB, tm + 2 * hal, FFN_SUB), F32),
            pltpu.VMEM((tm, d), F32),
            pltpu.SemaphoreType.DMA(()),
        ],
        compiler_params=_cparams(("arbitrary", "arbitrary", "arbitrary")),
        name="conv_glu_ffn",
    )(h2, h2, h2, x1, mod, modf, final_w.reshape(1, d), w_up, w_up, conv_w, conv_w, cb2, cb2, w_down)


def _pad_rows(x, rows):
    return jnp.concatenate([x, jnp.zeros((rows - x.shape[0],) + x.shape[1:], x.dtype)], axis=0)


def _pad_lanes(x, lanes):
    return jnp.concatenate([x, jnp.zeros(x.shape[:-1] + (lanes - x.shape[-1],), x.dtype)], axis=-1)


def _encoder(x, mod, modf, wts):
    rw, ssm = wts["rwkv"], wts["ssm"]
    xs_rw, p_ssm = _in_proj(x, mod, wts["norm1_w"], wts["w_in"], rw["mu"], ssm["conv_w"],
                            ssm["conv_b"], rw["ncol"], rw["width"], tm=256)
    nb = math.gcd(x.shape[0], SCAN_ROWS)
    yb = _rwkv_scan(True, xs_rw, rw, nb)
    y_rw = _rwkv_scan(False, xs_rw, rw, nb, yb)
    sb = _ssd_scan(True, p_ssm, ssm, nb)
    y_ssm = _ssd_scan(False, p_ssm, ssm, nb, sb)
    x1, h2 = _out_proj(x, y_rw, y_ssm, mod, wts["norm2_w"], wts["w_out"], tm=512)
    return _ffn(h2, x1, mod, modf, wts["final_norm_w"], wts["w_up"], wts["ffn_conv_w"],
                wts["ffn_conv_b"], wts["w_down"], tm=512, ck=512)


def kernel(x_prompt, x_sample, c_prompt, c_sample, norm1_w, w_in, rwkv_mu, rwkv_w0, rwkv_w2, rwkv_a0, rwkv_a2, rwkv_g2, rwkv_k_k, rwkv_k_a, rwkv_r_k, rwkv_lnx_w, rwkv_lnx_b, ssm_conv_w, ssm_conv_b, ssm_dt_bias, ssm_a_log, ssm_d, ssm_norm_w, w_out, norm2_w, ffn_w_up, ffn_conv_w, ffn_conv_b, ffn_w_down, w_ada, b_ada, final_norm_w, w_ada_final, b_ada_final):
    assert w_in.shape[0] == 1, "single-layer trunk"
    d = x_prompt.shape[-1]
    width = rwkv_w0.shape[-1]
    heads = width // HEAD
    lora = rwkv_w2.shape[2]
    n_rw = rwkv_mu.shape[-1]
    n_state = (ssm_conv_w.shape[-1] - width) // 4
    assert 2 * lora == LANES and rwkv_g2.shape[1] == LANES and 2 * heads <= LANES

    n_ssm = w_in.shape[2] - n_rw
    n_ssm_pad = -(-n_ssm // LANES) * LANES
    w_cat = _pad_lanes(w_in[0].astype(BF16), n_rw + n_ssm_pad)
    zl = jnp.zeros((lora, width), F32)
    eye_h = jnp.repeat(jnp.eye(heads, dtype=F32), HEAD, axis=1)
    seg = jnp.arange(MXU_DIM) // HEAD
    rwkv = {
        "width": width, "ncol": n_rw,
        "mu": rwkv_mu[0],
        "w0": [rwkv_w0[0, i:i + 1] for i in range(2)],
        "a0": [rwkv_a0[0, i:i + 1] for i in range(2)],
        "w2": [jnp.concatenate([rwkv_w2[0, 0], zl], 0).astype(BF16),
               jnp.concatenate([zl, rwkv_w2[0, 1]], 0).astype(BF16)],
        "a2": [jnp.concatenate([rwkv_a2[0, 0], zl], 0).astype(BF16),
               jnp.concatenate([zl, rwkv_a2[0, 1]], 0).astype(BF16)],
        "g2": rwkv_g2[0].astype(BF16),
        "k_k": rwkv_k_k, "k_a": rwkv_k_a, "r_k": rwkv_r_k[0].reshape(1, width),
        "lnx_w": rwkv_lnx_w, "lnx_b": rwkv_lnx_b,
        "bd": (seg[:, None] == seg[None, :]).astype(BF16),
    }
    ssm = {
        "width": width, "n_state": n_state,
        "conv_w": ssm_conv_w[0], "conv_b": ssm_conv_b,
        "dt_bias": _pad_lanes(ssm_dt_bias[0].reshape(1, 2 * heads), LANES),
        "a_log": _pad_lanes(ssm_a_log[0].reshape(1, 2 * heads), LANES),
        "d_skip": jnp.repeat(ssm_d[0], HEAD).reshape(1, width),
        "norm_w": ssm_norm_w,
        "expand": [_pad_rows(jnp.concatenate([eye_h * (1 - i), eye_h * i], 0), LANES).astype(BF16)
                   for i in range(2)],
    }
    wts = {
        "norm1_w": norm1_w[0], "norm2_w": norm2_w[0], "final_norm_w": final_norm_w,
        "w_in": w_cat, "w_out": w_out[0].astype(BF16),
        "w_up": ffn_w_up[0].astype(BF16), "w_down": ffn_w_down[0].astype(BF16),
        "ffn_conv_w": ffn_conv_w[0], "ffn_conv_b": ffn_conv_b[0],
        "rwkv": rwkv, "ssm": ssm,
    }

    nb_p, nb_s = c_prompt.shape[0], c_sample.shape[0]
    rows = -(-(nb_p + nb_s) // 8) * 8
    c_all = _pad_rows(jnp.concatenate([c_prompt, c_sample], 0), rows)
    mod_all = _ada(c_all, w_ada[0], b_ada[0]).reshape(rows, 6, d)
    modf_all = _ada(c_all, w_ada_final, b_ada_final).reshape(rows, 2, d)

    outs = []
    for x, lo, hi in ((x_prompt, 0, nb_p), (x_sample, nb_p, nb_p + nb_s)):
        outs.append(_encoder(x, mod_all[lo:hi], modf_all[lo:hi], wts))
    return tuple(outs)
```

```python
import functools
import math

import jax
import jax.numpy as jnp
from jax import lax
from jax.experimental import pallas as pl
from jax.experimental.pallas import tpu as pltpu

F32 = jnp.float32
BF16 = jnp.bfloat16

LANES = 128
SUBLANES_BF16 = 16
MXU_DIM = 256
VMEM_LIMIT_BYTES = 60 * 1024 * 1024

NORM_EPS = 1e-6
GN_EPS = 64e-5
GATED_EPS = 1e-5

HEAD = 64
PAIR = 2 * HEAD
RWKV_CHUNK = 64
SSD_CHUNK = 128
SCAN_ROWS = 4
HALO_ROWS = 8
INPROJ_BLOCK = MXU_DIM
FFN_SUB = MXU_DIM
NEG_BIG = -1e30


def _cparams(sem):
    return pltpu.CompilerParams(dimension_semantics=sem, vmem_limit_bytes=VMEM_LIMIT_BYTES)


def _dot(a, b):
    return jnp.dot(a, b, preferred_element_type=F32)


def _dot_nt(a, b):
    return lax.dot_general(a, b, (((1,), (1,)), ((), ())), preferred_element_type=F32)


def _sigmoid(x):
    return 1.0 / (1.0 + jnp.exp(-x))


def _silu(x):
    return x * _sigmoid(x)


def _split3(x):
    hi = x.astype(BF16)
    r1 = x - hi.astype(F32)
    mid = r1.astype(BF16)
    lo = (r1 - mid.astype(F32)).astype(BF16)
    return hi, mid, lo


def _cumsum_rows(x, tri3, n):
    hi, mid, lo = _split3(x)
    stacked = jnp.concatenate([hi, mid, lo], axis=0)
    out = _dot(tri3, stacked)
    return out[:n], out[n:n + 1]


def _make_tri3(n, reverse):
    rows = lax.broadcasted_iota(jnp.int32, (n + 8, 3 * n), 0)
    cols = lax.broadcasted_iota(jnp.int32, (n + 8, 3 * n), 1) & (n - 1)
    tri = (cols >= rows) if reverse else (cols <= rows)
    return jnp.where(tri | (rows >= n), 1.0, 0.0).astype(BF16)


def _segsum_heads(xs, bd):
    n = xs[0].shape[0]
    parts = []
    for x in xs:
        hi = x.astype(BF16)
        parts += [hi, (x - hi.astype(F32)).astype(BF16)]
    stacked = jnp.concatenate(parts, axis=0)
    out = jnp.concatenate(
        [_dot(stacked[:, i * MXU_DIM:(i + 1) * MXU_DIM], bd)
         for i in range(stacked.shape[1] // MXU_DIM)], axis=1)
    return [out[2 * i * n:(2 * i + 1) * n] + out[(2 * i + 1) * n:(2 * i + 2) * n]
            for i in range(len(xs))]


def _rms_mod(x, w, scale, shift):
    ms = jnp.mean(x * x, axis=-1, keepdims=True)
    return x * lax.rsqrt(ms + NORM_EPS) * w * (1.0 + scale) + shift


def _ada_kernel(c_ref, w_ref, b_ref, o_ref):
    c = c_ref[...]
    o_ref[...] = _dot(_silu(c), w_ref[...]) + b_ref[...]


def _ada(c_pad, w, b):
    rows, d = c_pad.shape
    n = w.shape[1]
    tn = 1024
    return pl.pallas_call(
        _ada_kernel,
        grid=(n // tn,),
        in_specs=[
            pl.BlockSpec((rows, d), lambda j: (0, 0)),
            pl.BlockSpec((d, tn), lambda j: (0, j)),
            pl.BlockSpec((1, tn), lambda j: (0, j)),
        ],
        out_specs=pl.BlockSpec((rows, tn), lambda j: (0, j)),
        out_shape=jax.ShapeDtypeStruct((rows, n), F32),
        compiler_params=_cparams(("arbitrary",)),
        name="ada_mod",
    )(c_pad, w, b.reshape(1, n))


def _inproj_kernel(tm, n_tiles, n_rw, width, x_ref, xp_ref, xn_ref, mod_ref, nw_ref, w_ref, mu_ref,
                   cw_ref, cb_ref, orw_ref, ossm_ref):
    j = pl.program_id(1)
    nw, scale, shift = nw_ref[...], mod_ref[0, 1:2, :], mod_ref[0, 0:1, :]
    h_prev = jnp.where(j > 0, _rms_mod(xp_ref[0], nw, scale, shift), 0.0)
    h_next = jnp.where(j < n_tiles - 1, _rms_mod(xn_ref[0], nw, scale, shift), 0.0)
    h = jnp.concatenate([h_prev, _rms_mod(x_ref[0], nw, scale, shift), h_next], axis=0).astype(BF16)
    rows = tm + 2 * HALO_ROWS
    mid = slice(HALO_ROWS, HALO_ROWS + tm)

    conv_k, conv_ch = cw_ref.shape
    left = (conv_k - 1) // 2
    n_ssm = ossm_ref.shape[2]

    def blocks(lo, hi):
        return [(c, min(c + INPROJ_BLOCK, hi)) for c in range(lo, hi, INPROJ_BLOCK)]

    def shift_tail(p, c0, c1):
        p_prev = pltpu.roll(p, 1, 0)
        p_next = pltpu.roll(p, rows - 1, 0)
        xs = p + mu_ref[0:1, c0:c1] * (p_prev - p) + mu_ref[1:2, c0:c1] * (p_next - p)
        orw_ref[0, :, c0:c1] = xs[mid]

    def copy_tail(p, c0, c1):
        ossm_ref[0, :, c0:c1] = p[mid]

    def conv_tail(ext, c0, c1):
        acc = cb_ref[:, c0:c1] + ext[mid] * cw_ref[left:left + 1, c0:c1]
        for i in range(conv_k):
            if i != left:
                shifted = pltpu.roll(ext, (left - i) % rows, 0)
                acc = acc + shifted[mid] * cw_ref[i:i + 1, c0:c1]
        ossm_ref[0, :, width + c0:width + c1] = _silu(acc)

    work = [(shift_tail, 0, c0, c1) for c0, c1 in blocks(0, n_rw)]
    work += [(conv_tail, n_rw + width, c0, c1) for c0, c1 in blocks(0, conv_ch)]
    work += [(copy_tail, n_rw, c0, c1) for c0, c1 in blocks(0, width) + blocks(width + conv_ch, n_ssm)]
    for tail, base, c0, c1 in work:
        tail(_dot(h, w_ref[:, base + c0:base + c1]), c0, c1)


def _in_proj(x, mod, norm_w, w_cat, mu, conv_w, conv_b, n_rw, width, tm):
    b, t, d = x.shape
    n_ssm = w_cat.shape[1] - n_rw
    n_tiles = t // tm
    hb = tm // HALO_ROWS
    last_hb = t // HALO_ROWS - 1
    return pl.pallas_call(
        functools.partial(_inproj_kernel, tm, n_tiles, n_rw, width),
        grid=(b, n_tiles),
        in_specs=[
            pl.BlockSpec((1, tm, d), lambda i, j: (i, j, 0)),
            pl.BlockSpec((1, HALO_ROWS, d), lambda i, j: (i, jnp.maximum(j * hb - 1, 0), 0)),
            pl.BlockSpec((1, HALO_ROWS, d), lambda i, j: (i, jnp.minimum((j + 1) * hb, last_hb), 0)),
            pl.BlockSpec((1, 6, d), lambda i, j: (i, 0, 0)),
            pl.BlockSpec((1, d), lambda i, j: (0, 0)),
            pl.BlockSpec(memory_space=pltpu.VMEM),
            pl.BlockSpec(mu.shape, lambda i, j: (0, 0)),
            pl.BlockSpec(conv_w.shape, lambda i, j: (0, 0)),
            pl.BlockSpec(conv_b.shape, lambda i, j: (0, 0)),
        ],
        out_specs=[
            pl.BlockSpec((1, tm, n_rw), lambda i, j: (i, j, 0)),
            pl.BlockSpec((1, tm, n_ssm), lambda i, j: (i, j, 0)),
        ],
        out_shape=[
            jax.ShapeDtypeStruct((b, t, n_rw), F32),
            jax.ShapeDtypeStruct((b, t, n_ssm), F32),
        ],
        compiler_params=_cparams(("arbitrary", "arbitrary")),
        name="in_proj",
    )(x, x, x, mod, norm_w.reshape(1, d), w_cat, mu, conv_w, conv_b)


def _stack2(x, m0):
    return jnp.concatenate([jnp.where(m0, x, 0.0), jnp.where(m0, 0.0, x)], axis=0)


def _rwkv_kernel(reverse, nb, width, *refs):
    if reverse:
        (xs_ref, w0_ref, w2_ref, a0_ref, a2_ref, kk_ref, ka_ref, bd_ref, y_ref, st_ref) = refs
    else:
        (xs_ref, w0_ref, w2_ref, a0_ref, a2_ref, kk_ref, ka_ref, bd_ref,
         a0o_ref, a2o_ref, g2_ref, rk_ref, lw_ref, lb_ref, yb_ref, y_ref, st_ref) = refs
    n = RWKV_CHUNK
    n_pairs = width // PAIR

    @pl.when(pl.program_id(1) == 0)
    def _():
        st_ref[...] = jnp.zeros_like(st_ref)

    bd = bd_ref[...]
    k_a = ka_ref[...]
    tri3 = _make_tri3(n, reverse)
    o = 3 * width

    def cols(c0, c1):
        return xs_ref[:, :, c0:c1].reshape(nb * n, c1 - c0)

    r, k, v = cols(0, width), cols(width, 2 * width), cols(2 * width, o)
    ad_b = cols(o + LANES, o + 2 * LANES).astype(BF16)
    w_log = w0_ref[...] + _dot(jnp.tanh(cols(o, o + LANES)).astype(BF16), w2_ref[...])
    lw = -math.exp(-0.5) * _sigmoid(w_log)
    a = _sigmoid(a0_ref[...] + _dot(ad_b, a2_ref[...]))
    kk = k * kk_ref[...]
    k_d = k * (1.0 + (a - 1.0) * k_a)
    if reverse:
        (kk_sq,) = _segsum_heads([kk * kk], bd)
    else:
        a_o = _sigmoid(a0o_ref[...] + _dot(ad_b, a2o_ref[...]))
        gate = _dot(_sigmoid(cols(o + 2 * LANES, o + 3 * LANES)).astype(BF16), g2_ref[...])
        ksum = k_d + k * (1.0 + (a_o - 1.0) * k_a)
        kk_sq, rk_sum = _segsum_heads([kk * kk, r * ksum * rk_ref[...]], bd)
    kk = kk / jnp.maximum(jnp.sqrt(kk_sq), 1e-12)
    beta = kk * a
    cums = [_cumsum_rows(lw[bi * n:(bi + 1) * n], tri3, n) for bi in range(nb)]
    c = jnp.concatenate([cu[0] for cu in cums], axis=0)
    ctot_rows = jnp.concatenate([jnp.broadcast_to(cu[1], (n, width)) for cu in cums], axis=0)
    e_nc = jnp.exp(-c)
    e_end = jnp.exp(ctot_rows - c)
    pre = dict(v=v, a_t=-kk * jnp.exp(c - lw), r_t=r * jnp.exp(c), b_t=beta * e_nc, k_t=k_d * e_nc,
               b_h=beta * e_end, k_h=k_d * e_end)
    p_end = [jnp.exp(cu[1]) for cu in cums]

    lane = lax.broadcasted_iota(jnp.int32, (1, PAIR), 1)
    m0 = lane < HEAD
    t_idx = lax.broadcasted_iota(jnp.int32, (n, 2 * n), 0)
    col = lax.broadcasted_iota(jnp.int32, (n, 2 * n), 1)
    s_idx = col & (n - 1)
    col_h0 = col < n
    if reverse:
        strict, incl = s_idx > t_idx, s_idx >= t_idx
    else:
        strict, incl = s_idx < t_idx, s_idx <= t_idx
    incl2 = jnp.concatenate([incl, incl], axis=1)
    eye2 = jnp.where(s_idx == t_idx, 1.0, 0.0)
    br = lax.broadcasted_iota(jnp.int32, (PAIR, PAIR), 0) < HEAD
    bc = lax.broadcasted_iota(jnp.int32, (PAIR, PAIR), 1) < HEAD
    blk = br == bc

    def blockdiag(x):
        return jnp.concatenate(
            [jnp.where(col_h0, x, 0.0), jnp.where(col_h0, 0.0, x)], axis=0).astype(BF16)

    units = [(bi, j) for bi in range(nb) for j in range(n_pairs)]
    ids = range(len(units))

    def col_of(name, u):
        bi, j = units[u]
        return pre[name][bi * n:(bi + 1) * n, j * PAIR:(j + 1) * PAIR]

    g1s, g2s = [], []
    for u in ids:
        lhs = jnp.concatenate([col_of("a_t", u), col_of("r_t", u)], axis=0).astype(BF16)
        rhs = jnp.concatenate([_stack2(col_of("b_t", u), m0), _stack2(col_of("k_t", u), m0)],
                              axis=0).astype(BF16)
        g1s.append(_dot_nt(lhs, rhs))
        g2s.append(_dot_nt(lhs, st_ref[u].astype(BF16)))
    a_ab = [jnp.where(strict, g1s[u][:n, :2 * n], 0.0) for u in ids]
    n_fac = int(math.log2(n))
    tmat = [eye2 + a_ab[u] for u in ids]
    pw = [_dot(a_ab[u].astype(BF16), blockdiag(a_ab[u])) for u in ids]
    for _ in range(1, n_fac - 1):
        both = [_dot(jnp.concatenate([tmat[u], pw[u]], axis=0).astype(BF16), blockdiag(pw[u]))
                for u in ids]
        tmat = [tmat[u] + both[u][:n] for u in ids]
        pw = [both[u][n:] for u in ids]
    tmat = [tmat[u] + _dot(tmat[u].astype(BF16), blockdiag(pw[u])) for u in ids]
    v2 = [_stack2(col_of("v", u), m0).astype(BF16) for u in ids]
    w_rhs = [g2s[u][:n] + _dot(jnp.where(strict, g1s[u][:n, 2 * n:], 0.0).astype(BF16), v2[u])
             for u in ids]
    us = [_dot(tmat[u].astype(BF16), _stack2(w_rhs[u], m0).astype(BF16)) for u in ids]
    ys = []
    for u in ids:
        a_r = jnp.where(incl2, g1s[u][n:], 0.0).astype(BF16)
        uv2 = jnp.concatenate([_stack2(us[u], m0).astype(BF16), v2[u]], axis=0)
        ys.append(g2s[u][n:] + _dot(a_r, uv2))
    for u in ids:
        uv_t = jnp.concatenate([us[u], col_of("v", u)], axis=0).T.astype(BF16)
        bk = jnp.concatenate([col_of("b_h", u), col_of("k_h", u)], axis=0).astype(BF16)
        bi, j = units[u]
        st_ref[u] = (st_ref[u] * p_end[bi][:, j * PAIR:(j + 1) * PAIR]
                     + jnp.where(blk, _dot(uv_t, bk), 0.0))

    y_all = jnp.concatenate(
        [jnp.concatenate(ys[bi * n_pairs:(bi + 1) * n_pairs], axis=1) for bi in range(nb)], axis=0)
    if reverse:
        y_ref[...] = y_all.reshape(nb, n, width)
    else:
        ysum = y_all + yb_ref[...].reshape(nb * n, width)
        inv = 1.0 / HEAD
        (mean,) = _segsum_heads([ysum], bd)
        dlt = ysum - mean * inv
        (var,) = _segsum_heads([dlt * dlt], bd)
        yn = dlt * lax.rsqrt(var * inv + GN_EPS) * lw_ref[...] + lb_ref[...]
        y_ref[...] = ((yn + rk_sum * v) * gate).astype(y_ref.dtype).reshape(nb, n, width)


def _rwkv_scan(reverse, xs, consts, nb, yb=None):
    b, t, ncol = xs.shape
    width = consts["width"]
    n = RWKV_CHUNK
    n_chunks = t // n

    def cidx(j):
        return (n_chunks - 1 - j) if reverse else j

    def row(x):
        return pl.BlockSpec((1, x.shape[1]), lambda i, j: (0, 0))

    def full(x):
        return pl.BlockSpec(x.shape, lambda i, j: (0,) * x.ndim)

    d = 1 if reverse else 0
    args = [xs, consts["w0"][d], consts["w2"][d], consts["a0"][d], consts["a2"][d], consts["k_k"],
            consts["k_a"], consts["bd"]]
    specs = [
        pl.BlockSpec((nb, n, ncol), lambda i, j: (i, cidx(j), 0)),
        row(consts["w0"][d]), full(consts["w2"][d]), row(consts["a0"][d]),
        full(consts["a2"][d]), row(consts["k_k"]), row(consts["k_a"]), full(consts["bd"]),
    ]
    if not reverse:
        args += [consts["a0"][1], consts["a2"][1], consts["g2"], consts["r_k"], consts["lnx_w"],
                 consts["lnx_b"], yb]
        specs += [row(consts["a0"][1]), full(consts["a2"][1]), full(consts["g2"]), row(consts["r_k"]),
                  row(consts["lnx_w"]), row(consts["lnx_b"]),
                  pl.BlockSpec((nb, n, width), lambda i, j: (i, cidx(j), 0))]
    out_dtype = F32 if reverse else BF16
    return pl.pallas_call(
        functools.partial(_rwkv_kernel, reverse, nb, width),
        grid=(b // nb, n_chunks),
        in_specs=specs,
        out_specs=pl.BlockSpec((nb, n, width), lambda i, j: (i, cidx(j), 0)),
        out_shape=jax.ShapeDtypeStruct((b, t, width), out_dtype),
        scratch_shapes=[pltpu.VMEM((nb * (width // PAIR), PAIR, PAIR), F32)],
        compiler_params=_cparams(("arbitrary", "arbitrary")),
        name="rwkv_bwd" if reverse else "rwkv_fwd",
    )(*args)


def _ssd_kernel(reverse, nb, width, n_state, *refs):
    if reverse:
        (p_ref, dtb_ref, alog_ref, ed_ref, y_ref, st_ref) = refs
    else:
        (p_ref, dtb_ref, alog_ref, ed_ref, dsk_ref, nw_ref, yb_ref, y_ref, st_ref) = refs
    q = SSD_CHUNK
    n_groups = 2
    conv_ch = width + 2 * n_groups * n_state
    c1 = width + conv_ch
    heads = width // HEAD
    n_pairs = heads // 2
    pairs_per_group = n_pairs // n_groups
    d = 1 if reverse else 0

    @pl.when(pl.program_id(1) == 0)
    def _():
        st_ref[...] = jnp.zeros_like(st_ref)

    tri3 = _make_tri3(q, reverse)
    ed = ed_ref[...]
    neg_a = -jnp.exp(alog_ref[...])

    small = []
    for bi in range(nb):
        z_dt = p_ref[bi, :, c1:c1 + LANES] + dtb_ref[...]
        dt = jnp.maximum(z_dt, 0.0) + jnp.log(1.0 + jnp.exp(-jnp.abs(z_dt)))
        acs, atot = _cumsum_rows(dt * neg_a, tri3, q)
        small.append((dt, acs, jnp.broadcast_to(atot, (HALO_ROWS, LANES))))
    rows_bi = 2 * q + HALO_ROWS
    terms = _split3(jnp.concatenate([x for s in small for x in s], axis=0))
    ex = _dot(jnp.concatenate(terms, axis=0), ed)
    n_rows = nb * rows_bi
    ex = ex[:n_rows] + ex[n_rows:2 * n_rows] + ex[2 * n_rows:]

    pre = []
    for bi in range(nb):
        base = bi * rows_bi
        dt_x, acs_x, atot_x = ex[base:base + q], ex[base + q:base + 2 * q], ex[base + 2 * q:base + 2 * q + 1]
        xs = p_ref[bi, :, width:2 * width]
        xd = xs * dt_x
        acs = small[bi][1]
        pre.append(dict(acs=acs, acs_t=acs.T, xs=xs, xd=xd, e_acs=jnp.exp(acs_x),
                        xdec=xd * jnp.exp(atot_x - acs_x), p_end=jnp.exp(atot_x)))

    t_idx = lax.broadcasted_iota(jnp.int32, (q, q), 0)
    s_idx = lax.broadcasted_iota(jnp.int32, (q, q), 1)
    incl = (s_idx >= t_idx) if reverse else (s_idx <= t_idx)
    lane = lax.broadcasted_iota(jnp.int32, (1, PAIR), 1)
    m0 = lane < HEAD

    groups = [(bi, g) for bi in range(nb) for g in range(n_groups)]
    cms, bmts, cbs = [], [], []
    for bi, g in groups:
        b0 = 2 * width + g * n_state
        c0 = 2 * width + (n_groups + g) * n_state
        bm = p_ref[bi, :, b0:b0 + n_state]
        cm_b = p_ref[bi, :, c0:c0 + n_state].astype(BF16)
        cms.append(cm_b)
        cbs.append(_dot_nt(cm_b, bm.astype(BF16)))
        bmts.append(bm.T.astype(BF16))
    units = [(bi, j) for bi in range(nb) for j in range(n_pairs)]
    ys = []
    for u, (bi, j) in enumerate(units):
        pb = pre[bi]
        gi = bi * n_groups + j // pairs_per_group
        sl = slice(j * PAIR, (j + 1) * PAIR)
        ms = []
        for hh in range(2):
            li = d * heads + 2 * j + hh
            diff = pb["acs"][:, li:li + 1] - pb["acs_t"][li:li + 1, :]
            ms.append((cbs[gi] * jnp.exp(jnp.where(incl, diff, NEG_BIG))).astype(BF16))
        y_diag = _dot(jnp.concatenate(ms, axis=1), _stack2(pb["xd"][:, sl], m0).astype(BF16))
        y_off = _dot(cms[gi], st_ref[u].astype(BF16)) * pb["e_acs"][:, sl]
        ys.append(y_diag + y_off)
    for u, (bi, j) in enumerate(units):
        pb = pre[bi]
        gi = bi * n_groups + j // pairs_per_group
        sl = slice(j * PAIR, (j + 1) * PAIR)
        st_ref[u] = st_ref[u] * pb["p_end"][:, sl] + _dot(bmts[gi], pb["xdec"][:, sl].astype(BF16))

    for bi in range(nb):
        y_all = jnp.concatenate(ys[bi * n_pairs:(bi + 1) * n_pairs], axis=1)
        gate = _silu(p_ref[bi, :, 0:width])
        if reverse:
            y_ref[bi] = y_all * gate
        else:
            y = (y_all + pre[bi]["xs"] * dsk_ref[...]) * gate + yb_ref[bi]
            ms = jnp.mean(y * y, axis=-1, keepdims=True)
            y_ref[bi] = (y * lax.rsqrt(ms + GATED_EPS) * nw_ref[...]).astype(y_ref.dtype)


def _ssd_scan(reverse, p_ssm, consts, nb, yb=None):
    b, t, ncol = p_ssm.shape
    width = consts["width"]
    n_state = consts["n_state"]
    q = SSD_CHUNK
    n_chunks = t // q

    def cidx(j):
        return (n_chunks - 1 - j) if reverse else j

    def row(x):
        return pl.BlockSpec((1, x.shape[1]), lambda i, j: (0, 0))

    def full(x):
        return pl.BlockSpec(x.shape, lambda i, j: (0,) * x.ndim)

    expand = consts["expand"][1 if reverse else 0]
    args = [p_ssm, consts["dt_bias"], consts["a_log"], expand]
    specs = [
        pl.BlockSpec((nb, q, ncol), lambda i, j: (i, cidx(j), 0)),
        row(consts["dt_bias"]), row(consts["a_log"]), full(expand),
    ]
    if not reverse:
        args += [consts["d_skip"], consts["norm_w"], yb]
        specs += [row(consts["d_skip"]), row(consts["norm_w"]),
                  pl.BlockSpec((nb, q, width), lambda i, j: (i, cidx(j), 0))]
    out_dtype = F32 if reverse else BF16
    return pl.pallas_call(
        functools.partial(_ssd_kernel, reverse, nb, width, n_state),
        grid=(b // nb, n_chunks),
        in_specs=specs,
        out_specs=pl.BlockSpec((nb, q, width), lambda i, j: (i, cidx(j), 0)),
        out_shape=jax.ShapeDtypeStruct((b, t, width), out_dtype),
        scratch_shapes=[pltpu.VMEM((nb * (width // PAIR), n_state, PAIR), F32)],
        compiler_params=_cparams(("arbitrary", "arbitrary")),
        name="ssd_bwd" if reverse else "ssd_fwd",
    )(*args)


def _outproj_kernel(half, x_ref, yr_ref, ys_ref, mod_ref, nw_ref, w_ref, x1_ref, h2_ref):
    attn = _dot(yr_ref[0], w_ref[:half, :]) + _dot(ys_ref[0], w_ref[half:, :])
    x1 = x_ref[0] + mod_ref[0, 2:3, :] * attn
    x1_ref[0] = x1
    h2_ref[0] = _rms_mod(x1, nw_ref[...], mod_ref[0, 4:5, :], mod_ref[0, 3:4, :]).astype(BF16)


def _out_proj(x, y_rw, y_ssm, mod, norm_w, w_out, tm):
    b, t, d = x.shape
    half = y_rw.shape[2]
    return pl.pallas_call(
        functools.partial(_outproj_kernel, half),
        grid=(b, t // tm),
        in_specs=[
            pl.BlockSpec((1, tm, d), lambda i, j: (i, j, 0)),
            pl.BlockSpec((1, tm, half), lambda i, j: (i, j, 0)),
            pl.BlockSpec((1, tm, half), lambda i, j: (i, j, 0)),
            pl.BlockSpec((1, 6, d), lambda i, j: (i, 0, 0)),
            pl.BlockSpec((1, d), lambda i, j: (0, 0)),
            pl.BlockSpec(memory_space=pltpu.VMEM),
        ],
        out_specs=[
            pl.BlockSpec((1, tm, d), lambda i, j: (i, j, 0)),
            pl.BlockSpec((1, tm, d), lambda i, j: (i, j, 0)),
        ],
        out_shape=[
            jax.ShapeDtypeStruct((b, t, d), F32),
            jax.ShapeDtypeStruct((b, t, d), BF16),
        ],
        compiler_params=_cparams(("arbitrary", "arbitrary")),
        name="out_proj",
    )(x, y_rw, y_ssm, mod, norm_w.reshape(1, d), w_out)


def _ffn_kernel(tm, n_tiles, n_ff, h_ref, hp_ref, hn_ref, x1_hbm, mod_ref, modf_ref, fw_ref,
                wv_ref, wg_ref, cwv_ref, cwg_ref, cbv_ref, cbg_ref, wd_ref,
                o_ref, lhs_ref, uv_ref, ug_ref, x1_ref, x1_sem):
    i = pl.program_id(1)
    f = pl.program_id(2)
    hal = SUBLANES_BF16
    x1_copy = pltpu.make_async_copy(
        x1_hbm.at[pl.program_id(0), pl.ds(i * tm, tm), :], x1_ref, x1_sem)

    @pl.when(f == 0)
    def _():
        x1_copy.start()
        lhs_ref[0:hal, :] = jnp.where(i > 0, hp_ref[0], jnp.zeros_like(hp_ref[0]))
        lhs_ref[hal:hal + tm, :] = h_ref[0]
        lhs_ref[hal + tm:, :] = jnp.where(i < n_tiles - 1, hn_ref[0], jnp.zeros_like(hn_ref[0]))
        o_ref[0] = jnp.zeros(o_ref.shape[1:], F32)

    lhs = lhs_ref[...]
    ck = wv_ref.shape[1]
    subs = [(c, c + FFN_SUB) for c in range(0, ck, FFN_SUB)]
    for s, (c0, c1) in enumerate(subs):
        uv_ref[s] = _dot(lhs, wv_ref[:, c0:c1])
    for s, (c0, c1) in enumerate(subs):
        ug_ref[s] = _dot(lhs, wg_ref[:, c0:c1])

    def conv(u_ref, s, cw_ref, cb_ref, c0, c1):
        out = cb_ref[:, c0:c1] + u_ref[s, pl.ds(hal - 1, tm), :] * cw_ref[0:1, c0:c1]
        out = out + u_ref[s, pl.ds(hal, tm), :] * cw_ref[1:2, c0:c1]
        return out + u_ref[s, pl.ds(hal + 1, tm), :] * cw_ref[2:3, c0:c1]

    for s, (c0, c1) in enumerate(subs):
        act = _silu(conv(uv_ref, s, cwv_ref, cbv_ref, c0, c1)) * conv(ug_ref, s, cwg_ref, cbg_ref, c0, c1)
        o_ref[0] += _dot(act.astype(BF16), wd_ref[c0:c1, :])

    @pl.when(f == n_ff - 1)
    def _():
        x1_copy.wait()
        x2 = x1_ref[...] + mod_ref[0, 5:6, :] * o_ref[0]
        o_ref[0] = _rms_mod(x2, fw_ref[...], modf_ref[0, 1:2, :], modf_ref[0, 0:1, :])


def _ffn(h2, x1, mod, modf, final_w, w_up, conv_w, conv_b, w_down, tm, ck):
    b, t, d = x1.shape
    d_ff = w_down.shape[0]
    n_ff = d_ff // ck
    n_tiles = t // tm
    hal = SUBLANES_BF16
    hb = tm // hal
    last_hb = t // hal - 1
    cb2 = conv_b.reshape(1, 2 * d_ff)
    return pl.pallas_call(
        functools.partial(_ffn_kernel, tm, n_tiles, n_ff),
        grid=(b, n_tiles, n_ff),
        in_specs=[
            pl.BlockSpec((1, tm, d), lambda i, j, f: (i, j, 0)),
            pl.BlockSpec((1, hal, d), lambda i, j, f: (i, jnp.maximum(j * hb - 1, 0), 0)),
            pl.BlockSpec((1, hal, d), lambda i, j, f: (i, jnp.minimum((j + 1) * hb, last_hb), 0)),
            pl.BlockSpec(memory_space=pl.ANY),
            pl.BlockSpec((1, 6, d), lambda i, j, f: (i, 0, 0)),
            pl.BlockSpec((1, 2, d), lambda i, j, f: (i, 0, 0)),
            pl.BlockSpec((1, d), lambda i, j, f: (0, 0)),
            pl.BlockSpec((d, ck), lambda i, j, f: (0, f)),
            pl.BlockSpec((d, ck), lambda i, j, f: (0, f + n_ff)),
            pl.BlockSpec((conv_w.shape[0], ck), lambda i, j, f: (0, f)),
            pl.BlockSpec((conv_w.shape[0], ck), lambda i, j, f: (0, f + n_ff)),
            pl.BlockSpec((1, ck), lambda i, j, f: (0, f)),
            pl.BlockSpec((1, ck), lambda i, j, f: (0, f + n_ff)),
            pl.BlockSpec((ck, d), lambda i, j, f: (f, 0)),
        ],
        out_specs=pl.BlockSpec((1, tm, d), lambda i, j, f: (i, j, 0)),
        out_shape=jax.ShapeDtypeStruct((b, t, d), F32),
        scratch_shapes=[
            pltpu.VMEM((tm + 2 * hal, d), BF16),
            pltpu.VMEM((ck // FFN_SUB, tm + 2 * hal, FFN_SUB), F32),
            pltpu.VMEM((ck // FFN_SUB, tm + 2 * hal, FFN_SUB), F32),
            pltpu.VMEM((tm, d), F32),
            pltpu.SemaphoreType.DMA(()),
        ],
        compiler_params=_cparams(("arbitrary", "arbitrary", "arbitrary")),
        name="conv_glu_ffn",
    )(h2, h2, h2, x1, mod, modf, final_w.reshape(1, d), w_up, w_up, conv_w, conv_w, cb2, cb2, w_down)


def _pad_rows(x, rows):
    return jnp.concatenate([x, jnp.zeros((rows - x.shape[0],) + x.shape[1:], x.dtype)], axis=0)


def _pad_lanes(x, lanes):
    return jnp.concatenate([x, jnp.zeros(x.shape[:-1] + (lanes - x.shape[-1],), x.dtype)], axis=-1)


def _encoder(x, mod, modf, wts):
    rw, ssm = wts["rwkv"], wts["ssm"]
    xs_rw, p_ssm = _in_proj(x, mod, wts["norm1_w"], wts["w_in"], rw["mu"], ssm["conv_w"],
                            ssm["conv_b"], rw["ncol"], rw["width"], tm=256)
    nb = math.gcd(x.shape[0], SCAN_ROWS)
    yb = _rwkv_scan(True, xs_rw, rw, nb)
    y_rw = _rwkv_scan(False, xs_rw, rw, nb, yb)
    sb = _ssd_scan(True, p_ssm, ssm, nb)
    y_ssm = _ssd_scan(False, p_ssm, ssm, nb, sb)
    x1, h2 = _out_proj(x, y_rw, y_ssm, mod, wts["norm2_w"], wts["w_out"], tm=512)
    return _ffn(h2, x1, mod, modf, wts["final_norm_w"], wts["w_up"], wts["ffn_conv_w"],
                wts["ffn_conv_b"], wts["w_down"], tm=512, ck=512)


def kernel(x_prompt, x_sample, c_prompt, c_sample, norm1_w, w_in, rwkv_mu, rwkv_w0, rwkv_w2, rwkv_a0, rwkv_a2, rwkv_g2, rwkv_k_k, rwkv_k_a, rwkv_r_k, rwkv_lnx_w, rwkv_lnx_b, ssm_conv_w, ssm_conv_b, ssm_dt_bias, ssm_a_log, ssm_d, ssm_norm_w, w_out, norm2_w, ffn_w_up, ffn_conv_w, ffn_conv_b, ffn_w_down, w_ada, b_ada, final_norm_w, w_ada_final, b_ada_final):
    assert w_in.shape[0] == 1, "single-layer trunk"
    d = x_prompt.shape[-1]
    width = rwkv_w0.shape[-1]
    heads = width // HEAD
    lora = rwkv_w2.shape[2]
    n_rw = rwkv_mu.shape[-1]
    n_state = (ssm_conv_w.shape[-1] - width) // 4
    assert 2 * lora == LANES and rwkv_g2.shape[1] == LANES and 2 * heads <= LANES

    n_ssm = w_in.shape[2] - n_rw
    n_ssm_pad = -(-n_ssm // LANES) * LANES
    w_cat = _pad_lanes(w_in[0].astype(BF16), n_rw + n_ssm_pad)
    zl = jnp.zeros((lora, width), F32)
    eye_h = jnp.repeat(jnp.eye(heads, dtype=F32), HEAD, axis=1)
    seg = jnp.arange(MXU_DIM) // HEAD
    rwkv = {
        "width": width, "ncol": n_rw,
        "mu": rwkv_mu[0],
        "w0": [rwkv_w0[0, i:i + 1] for i in range(2)],
        "a0": [rwkv_a0[0, i:i + 1] for i in range(2)],
        "w2": [jnp.concatenate([rwkv_w2[0, 0], zl], 0).astype(BF16),
               jnp.concatenate([zl, rwkv_w2[0, 1]], 0).astype(BF16)],
        "a2": [jnp.concatenate([rwkv_a2[0, 0], zl], 0).astype(BF16),
               jnp.concatenate([zl, rwkv_a2[0, 1]], 0).astype(BF16)],
        "g2": rwkv_g2[0].astype(BF16),
        "k_k": rwkv_k_k, "k_a": rwkv_k_a, "r_k": rwkv_r_k[0].reshape(1, width),
        "lnx_w": rwkv_lnx_w, "lnx_b": rwkv_lnx_b,
        "bd": (seg[:, None] == seg[None, :]).astype(BF16),
    }
    ssm = {
        "width": width, "n_state": n_state,
        "conv_w": ssm_conv_w[0], "conv_b": ssm_conv_b,
        "dt_bias": _pad_lanes(ssm_dt_bias[0].reshape(1, 2 * heads), LANES),
        "a_log": _pad_lanes(ssm_a_log[0].reshape(1, 2 * heads), LANES),
        "d_skip": jnp.repeat(ssm_d[0], HEAD).reshape(1, width),
        "norm_w": ssm_norm_w,
        "expand": [_pad_rows(jnp.concatenate([eye_h * (1 - i), eye_h * i], 0), LANES).astype(BF16)
                   for i in range(2)],
    }
    wts = {
        "norm1_w": norm1_w[0], "norm2_w": norm2_w[0], "final_norm_w": final_norm_w,
        "w_in": w_cat, "w_out": w_out[0].astype(BF16),
        "w_up": ffn_w_up[0].astype(BF16), "w_down": ffn_w_down[0].astype(BF16),
        "ffn_conv_w": ffn_conv_w[0], "ffn_conv_b": ffn_conv_b[0],
        "rwkv": rwkv, "ssm": ssm,
    }

    nb_p, nb_s = c_prompt.shape[0], c_sample.shape[0]
    rows = -(-(nb_p + nb_s) // 8) * 8
    c_all = _pad_rows(jnp.concatenate([c_prompt, c_sample], 0), rows)
    mod_all = _ada(c_all, w_ada[0], b_ada[0]).reshape(rows, 6, d)
    modf_all = _ada(c_all, w_ada_final, b_ada_final).reshape(rows, 2, d)

    outs = []
    for x, lo, hi in ((x_prompt, 0, nb_p), (x_sample, nb_p, nb_p + nb_s)):
        outs.append(_encoder(x, mod_all[lo:hi], modf_all[lo:hi], wts))
    return tuple(outs)
```

```python
import functools
import math

import jax
import jax.numpy as jnp
from jax import lax
from jax.experimental import pallas as pl
from jax.experimental.pallas import tpu as pltpu

F32 = jnp.float32
BF16 = jnp.bfloat16

LANES = 128
SUBLANES_BF16 = 16
MXU_DIM = 256
VMEM_LIMIT_BYTES = 60 * 1024 * 1024

NORM_EPS = 1e-6
GN_EPS = 64e-5
GATED_EPS = 1e-5

HEAD = 64
PAIR = 2 * HEAD
RWKV_CHUNK = 64
SSD_CHUNK = 128
SCAN_ROWS = 4
HALO_ROWS = 8
INPROJ_BLOCK = MXU_DIM
FFN_SUB = MXU_DIM
NEG_BIG = -1e30


def _cparams(sem):
    return pltpu.CompilerParams(dimension_semantics=sem, vmem_limit_bytes=VMEM_LIMIT_BYTES)


def _dot(a, b):
    return jnp.dot(a, b, preferred_element_type=F32)


def _dot_nt(a, b):
    return lax.dot_general(a, b, (((1,), (1,)), ((), ())), preferred_element_type=F32)


def _sigmoid(x):
    return 1.0 / (1.0 + jnp.exp(-x))


def _silu(x):
    return x * _sigmoid(x)


def _split3(x):
    hi = x.astype(BF16)
    r1 = x - hi.astype(F32)
    mid = r1.astype(BF16)
    lo = (r1 - mid.astype(F32)).astype(BF16)
    return hi, mid, lo


def _cumsum_rows(x, tri3, n):
    hi, mid, lo = _split3(x)
    stacked = jnp.concatenate([hi, mid, lo], axis=0)
    out = _dot(tri3, stacked)
    return out[:n], out[n:n + 1]


def _make_tri3(n, reverse):
    rows = lax.broadcasted_iota(jnp.int32, (n + 8, 3 * n), 0)
    cols = lax.broadcasted_iota(jnp.int32, (n + 8, 3 * n), 1) & (n - 1)
    tri = (cols >= rows) if reverse else (cols <= rows)
    return jnp.where(tri | (rows >= n), 1.0, 0.0).astype(BF16)


def _segsum_heads(xs, bd):
    n = xs[0].shape[0]
    parts = []
    for x in xs:
        hi = x.astype(BF16)
        parts += [hi, (x - hi.astype(F32)).astype(BF16)]
    stacked = jnp.concatenate(parts, axis=0)
    out = jnp.concatenate(
        [_dot(stacked[:, i * MXU_DIM:(i + 1) * MXU_DIM], bd)
         for i in range(stacked.shape[1] // MXU_DIM)], axis=1)
    return [out[2 * i * n:(2 * i + 1) * n] + out[(2 * i + 1) * n:(2 * i + 2) * n]
            for i in range(len(xs))]


def _rms_mod(x, w, scale, shift):
    ms = jnp.mean(x * x, axis=-1, keepdims=True)
    return x * lax.rsqrt(ms + NORM_EPS) * w * (1.0 + scale) + shift


def _ada_kernel(c_ref, w_ref, b_ref, o_ref):
    c = c_ref[...]
    o_ref[...] = _dot(_silu(c), w_ref[...]) + b_ref[...]


def _ada(c_pad, w, b):
    rows, d = c_pad.shape
    n = w.shape[1]
    tn = 1024
    return pl.pallas_call(
        _ada_kernel,
        grid=(n // tn,),
        in_specs=[
            pl.BlockSpec((rows, d), lambda j: (0, 0)),
            pl.BlockSpec((d, tn), lambda j: (0, j)),
            pl.BlockSpec((1, tn), lambda j: (0, j)),
        ],
        out_specs=pl.BlockSpec((rows, tn), lambda j: (0, j)),
        out_shape=jax.ShapeDtypeStruct((rows, n), F32),
        compiler_params=_cparams(("arbitrary",)),
        name="ada_mod",
    )(c_pad, w, b.reshape(1, n))


def _inproj_kernel(tm, n_tiles, n_rw, width, x_ref, xp_ref, xn_ref, mod_ref, nw_ref, w_ref, mu_ref,
                   cw_ref, cb_ref, orw_ref, ossm_ref):
    j = pl.program_id(1)
    nw, scale, shift = nw_ref[...], mod_ref[0, 1:2, :], mod_ref[0, 0:1, :]
    h_prev = jnp.where(j > 0, _rms_mod(xp_ref[0], nw, scale, shift), 0.0)
    h_next = jnp.where(j < n_tiles - 1, _rms_mod(xn_ref[0], nw, scale, shift), 0.0)
    h = jnp.concatenate([h_prev, _rms_mod(x_ref[0], nw, scale, shift), h_next], axis=0).astype(BF16)
    rows = tm + 2 * HALO_ROWS
    mid = slice(HALO_ROWS, HALO_ROWS + tm)

    conv_k, conv_ch = cw_ref.shape
    left = (conv_k - 1) // 2
    n_ssm = ossm_ref.shape[2]

    def blocks(lo, hi):
        return [(c, min(c + INPROJ_BLOCK, hi)) for c in range(lo, hi, INPROJ_BLOCK)]

    def shift_tail(p, c0, c1):
        p_prev = pltpu.roll(p, 1, 0)
        p_next = pltpu.roll(p, rows - 1, 0)
        xs = p + mu_ref[0:1, c0:c1] * (p_prev - p) + mu_ref[1:2, c0:c1] * (p_next - p)
        orw_ref[0, :, c0:c1] = xs[mid]

    def copy_tail(p, c0, c1):
        ossm_ref[0, :, c0:c1] = p[mid]

    def conv_tail(ext, c0, c1):
        acc = cb_ref[:, c0:c1] + ext[mid] * cw_ref[left:left + 1, c0:c1]
        for i in range(conv_k):
            if i != left:
                shifted = pltpu.roll(ext, (left - i) % rows, 0)
                acc = acc + shifted[mid] * cw_ref[i:i + 1, c0:c1]
        ossm_ref[0, :, width + c0:width + c1] = _silu(acc)

    work = [(shift_tail, 0, c0, c1) for c0, c1 in blocks(0, n_rw)]
    work += [(conv_tail, n_rw + width, c0, c1) for c0, c1 in blocks(0, conv_ch)]
    work += [(copy_tail, n_rw, c0, c1) for c0, c1 in blocks(0, width) + blocks(width + conv_ch, n_ssm)]
    for tail, base, c0, c1 in work:
        tail(_dot(h, w_ref[:, base + c0:base + c1]), c0, c1)


def _in_proj(x, mod, norm_w, w_cat, mu, conv_w, conv_b, n_rw, width, tm):
    b, t, d = x.shape
    n_ssm = w_cat.shape[1] - n_rw
    n_tiles = t // tm
    hb = tm // HALO_ROWS
    last_hb = t // HALO_ROWS - 1
    return pl.pallas_call(
        functools.partial(_inproj_kernel, tm, n_tiles, n_rw, width),
        grid=(b, n_tiles),
        in_specs=[
            pl.BlockSpec((1, tm, d), lambda i, j: (i, j, 0)),
            pl.BlockSpec((1, HALO_ROWS, d), lambda i, j: (i, jnp.maximum(j * hb - 1, 0), 0)),
            pl.BlockSpec((1, HALO_ROWS, d), lambda i, j: (i, jnp.minimum((j + 1) * hb, last_hb), 0)),
            pl.BlockSpec((1, 6, d), lambda i, j: (i, 0, 0)),
            pl.BlockSpec((1, d), lambda i, j: (0, 0)),
            pl.BlockSpec(memory_space=pltpu.VMEM),
            pl.BlockSpec(mu.shape, lambda i, j: (0, 0)),
            pl.BlockSpec(conv_w.shape, lambda i, j: (0, 0)),
            pl.BlockSpec(conv_b.shape, lambda i, j: (0, 0)),
        ],
        out_specs=[
            pl.BlockSpec((1, tm, n_rw), lambda i, j: (i, j, 0)),
            pl.BlockSpec((1, tm, n_ssm), lambda i, j: (i, j, 0)),
        ],
        out_shape=[
            jax.ShapeDtypeStruct((b, t, n_rw), F32),
            jax.ShapeDtypeStruct((b, t, n_ssm), F32),
        ],
        compiler_params=_cparams(("arbitrary", "arbitrary")),
        name="in_proj",
    )(x, x, x, mod, norm_w.reshape(1, d), w_cat, mu, conv_w, conv_b)


def _stack2(x, m0):
    return jnp.concatenate([jnp.where(m0, x, 0.0), jnp.where(m0, 0.0, x)], axis=0)


def _rwkv_kernel(reverse, nb, width, *refs):
    if reverse:
        (xs_ref, w0_ref, w2_ref, a0_ref, a2_ref, kk_ref, ka_ref, bd_ref, y_ref, st_ref) = refs
    else:
        (xs_ref, w0_ref, w2_ref, a0_ref, a2_ref, kk_ref, ka_ref, bd_ref,
         a0o_ref, a2o_ref, g2_ref, rk_ref, lw_ref, lb_ref, yb_ref, y_ref, st_ref) = refs
    n = RWKV_CHUNK
    n_pairs = width // PAIR

    @pl.when(pl.program_id(1) == 0)
    def _():
        st_ref[...] = jnp.zeros_like(st_ref)

    bd = bd_ref[...]
    k_a = ka_ref[...]
    tri3 = _make_tri3(n, reverse)
    o = 3 * width

    def cols(c0, c1):
        return xs_ref[:, :, c0:c1].reshape(nb * n, c1 - c0)

    r, k, v = cols(0, width), cols(width, 2 * width), cols(2 * width, o)
    ad_b = cols(o + LANES, o + 2 * LANES).astype(BF16)
    w_log = w0_ref[...] + _dot(jnp.tanh(cols(o, o + LANES)).astype(BF16), w2_ref[...])
    lw = -math.exp(-0.5) * _sigmoid(w_log)
    a = _sigmoid(a0_ref[...] + _dot(ad_b, a2_ref[...]))
    kk = k * kk_ref[...]
    k_d = k * (1.0 + (a - 1.0) * k_a)
    if reverse:
        (kk_sq,) = _segsum_heads([kk * kk], bd)
    else:
        a_o = _sigmoid(a0o_ref[...] + _dot(ad_b, a2o_ref[...]))
        gate = _dot(_sigmoid(cols(o + 2 * LANES, o + 3 * LANES)).astype(BF16), g2_ref[...])
        ksum = k_d + k * (1.0 + (a_o - 1.0) * k_a)
        kk_sq, rk_sum = _segsum_heads([kk * kk, r * ksum * rk_ref[...]], bd)
    kk = kk / jnp.maximum(jnp.sqrt(kk_sq), 1e-12)
    beta = kk * a
    cums = [_cumsum_rows(lw[bi * n:(bi + 1) * n], tri3, n) for bi in range(nb)]
    c = jnp.concatenate([cu[0] for cu in cums], axis=0)
    ctot_rows = jnp.concatenate([jnp.broadcast_to(cu[1], (n, width)) for cu in cums], axis=0)
    e_nc = jnp.exp(-c)
    e_end = jnp.exp(ctot_rows - c)
    pre = dict(v=v, a_t=-kk * jnp.exp(c - lw), r_t=r * jnp.exp(c), b_t=beta * e_nc, k_t=k_d * e_nc,
               b_h=beta * e_end, k_h=k_d * e_end)
    p_end = [jnp.exp(cu[1]) for cu in cums]

    lane = lax.broadcasted_iota(jnp.int32, (1, PAIR), 1)
    m0 = lane < HEAD
    t_idx = lax.broadcasted_iota(jnp.int32, (n, 2 * n), 0)
    col = lax.broadcasted_iota(jnp.int32, (n, 2 * n), 1)
    s_idx = col & (n - 1)
    col_h0 = col < n
    if reverse:
        strict, incl = s_idx > t_idx, s_idx >= t_idx
    else:
        strict, incl = s_idx < t_idx, s_idx <= t_idx
    incl2 = jnp.concatenate([incl, incl], axis=1)
    eye2 = jnp.where(s_idx == t_idx, 1.0, 0.0)
    br = lax.broadcasted_iota(jnp.int32, (PAIR, PAIR), 0) < HEAD
    bc = lax.broadcasted_iota(jnp.int32, (PAIR, PAIR), 1) < HEAD
    blk = br == bc

    def blockdiag(x):
        return jnp.concatenate(
            [jnp.where(col_h0, x, 0.0), jnp.where(col_h0, 0.0, x)], axis=0).astype(BF16)

    units = [(bi, j) for bi in range(nb) for j in range(n_pairs)]
    ids = range(len(units))

    def col_of(name, u):
        bi, j = units[u]
        return pre[name][bi * n:(bi + 1) * n, j * PAIR:(j + 1) * PAIR]

    g1s, g2s = [], []
    for u in ids:
        lhs = jnp.concatenate([col_of("a_t", u), col_of("r_t", u)], axis=0).astype(BF16)
        rhs = jnp.concatenate([_stack2(col_of("b_t", u), m0), _stack2(col_of("k_t", u), m0)],
                              axis=0).astype(BF16)
        g1s.append(_dot_nt(lhs, rhs))
        g2s.append(_dot_nt(lhs, st_ref[u].astype(BF16)))
    a_ab = [jnp.where(strict, g1s[u][:n, :2 * n], 0.0) for u in ids]
    n_fac = int(math.log2(n))
    tmat = [eye2 + a_ab[u] for u in ids]
    pw = [_dot(a_ab[u].astype(BF16), blockdiag(a_ab[u])) for u in ids]
    for _ in range(1, n_fac - 1):
        both = [_dot(jnp.concatenate([tmat[u], pw[u]], axis=0).astype(BF16), blockdiag(pw[u]))
                for u in ids]
        tmat = [tmat[u] + both[u][:n] for u in ids]
        pw = [both[u][n:] for u in ids]
    tmat = [tmat[u] + _dot(tmat[u].astype(BF16), blockdiag(pw[u])) for u in ids]
    v2 = [_stack2(col_of("v", u), m0).astype(BF16) for u in ids]
    w_rhs = [g2s[u][:n] + _dot(jnp.where(strict, g1s[u][:n, 2 * n:], 0.0).astype(BF16), v2[u])
             for u in ids]
    us = [_dot(tmat[u].astype(BF16), _stack2(w_rhs[u], m0).astype(BF16)) for u in ids]
    ys = []
    for u in ids:
        a_r = jnp.where(incl2, g1s[u][n:], 0.0).astype(BF16)
        uv2 = jnp.concatenate([_stack2(us[u], m0).astype(BF16), v2[u]], axis=0)
        ys.append(g2s[u][n:] + _dot(a_r, uv2))
    for u in ids:
        uv_t = jnp.concatenate([us[u], col_of("v", u)], axis=0).T.astype(BF16)
        bk = jnp.concatenate([col_of("b_h", u), col_of("k_h", u)], axis=0).astype(BF16)
        bi, j = units[u]
        st_ref[u] = (st_ref[u] * p_end[bi][:, j * PAIR:(j + 1) * PAIR]
                     + jnp.where(blk, _dot(uv_t, bk), 0.0))

    y_all = jnp.concatenate(
        [jnp.concatenate(ys[bi * n_pairs:(bi + 1) * n_pairs], axis=1) for bi in range(nb)], axis=0)
    if reverse:
        y_ref[...] = y_all.reshape(nb, n, width)
    else:
        ysum = y_all + yb_ref[...].reshape(nb * n, width)
        inv = 1.0 / HEAD
        (mean,) = _segsum_heads([ysum], bd)
        dlt = ysum - mean * inv
        (var,) = _segsum_heads([dlt * dlt], bd)
        yn = dlt * lax.rsqrt(var * inv + GN_EPS) * lw_ref[...] + lb_ref[...]
        y_ref[...] = ((yn + rk_sum * v) * gate).astype(y_ref.dtype).reshape(nb, n, width)


def _rwkv_scan(reverse, xs, consts, nb, yb=None):
    b, t, ncol = xs.shape
    width = consts["width"]
    n = RWKV_CHUNK
    n_chunks = t // n

    def cidx(j):
        return (n_chunks - 1 - j) if reverse else j

    def row(x):
        return pl.BlockSpec((1, x.shape[1]), lambda i, j: (0, 0))

    def full(x):
        return pl.BlockSpec(x.shape, lambda i, j: (0,) * x.ndim)

    d = 1 if reverse else 0
    args = [xs, consts["w0"][d], consts["w2"][d], consts["a0"][d], consts["a2"][d], consts["k_k"],
            consts["k_a"], consts["bd"]]
    specs = [
        pl.BlockSpec((nb, n, ncol), lambda i, j: (i, cidx(j), 0)),
        row(consts["w0"][d]), full(consts["w2"][d]), row(consts["a0"][d]),
        full(consts["a2"][d]), row(consts["k_k"]), row(consts["k_a"]), full(consts["bd"]),
    ]
    if not reverse:
        args += [consts["a0"][1], consts["a2"][1], consts["g2"], consts["r_k"], consts["lnx_w"],
                 consts["lnx_b"], yb]
        specs += [row(consts["a0"][1]), full(consts["a2"][1]), full(consts["g2"]), row(consts["r_k"]),
                  row(consts["lnx_w"]), row(consts["lnx_b"]),
                  pl.BlockSpec((nb, n, width), lambda i, j: (i, cidx(j), 0))]
    out_dtype = F32 if reverse else BF16
    return pl.pallas_call(
        functools.partial(_rwkv_kernel, reverse, nb, width),
        grid=(b // nb, n_chunks),
        in_specs=specs,
        out_specs=pl.BlockSpec((nb, n, width), lambda i, j: (i, cidx(j), 0)),
        out_shape=jax.ShapeDtypeStruct((b, t, width), out_dtype),
        scratch_shapes=[pltpu.VMEM((nb * (width // PAIR), PAIR, PAIR), F32)],
        compiler_params=_cparams(("arbitrary", "arbitrary")),
        name="rwkv_bwd" if reverse else "rwkv_fwd",
    )(*args)


def _ssd_kernel(reverse, nb, width, n_state, *refs):
    if reverse:
        (p_ref, dtb_ref, alog_ref, ed_ref, y_ref, st_ref) = refs
    else:
        (p_ref, dtb_ref, alog_ref, ed_ref, dsk_ref, nw_ref, yb_ref, y_ref, st_ref) = refs
    q = SSD_CHUNK
    n_groups = 2
    conv_ch = width + 2 * n_groups * n_state
    c1 = width + conv_ch
    heads = width // HEAD
    n_pairs = heads // 2
    pairs_per_group = n_pairs // n_groups
    d = 1 if reverse else 0

    @pl.when(pl.program_id(1) == 0)
    def _():
        st_ref[...] = jnp.zeros_like(st_ref)

    tri3 = _make_tri3(q, reverse)
    ed = ed_ref[...]
    neg_a = -jnp.exp(alog_ref[...])

    small = []
    for bi in range(nb):
        z_dt = p_ref[bi, :, c1:c1 + LANES] + dtb_ref[...]
        dt = jnp.maximum(z_dt, 0.0) + jnp.log(1.0 + jnp.exp(-jnp.abs(z_dt)))
        acs, atot = _cumsum_rows(dt * neg_a, tri3, q)
        small.append((dt, acs, jnp.broadcast_to(atot, (HALO_ROWS, LANES))))
    rows_bi = 2 * q + HALO_ROWS
    terms = _split3(jnp.concatenate([x for s in small for x in s], axis=0))
    ex = _dot(jnp.concatenate(terms, axis=0), ed)
    n_rows = nb * rows_bi
    ex = ex[:n_rows] + ex[n_rows:2 * n_rows] + ex[2 * n_rows:]

    pre = []
    for bi in range(nb):
        base = bi * rows_bi
        dt_x, acs_x, atot_x = ex[base:base + q], ex[base + q:base + 2 * q], ex[base + 2 * q:base + 2 * q + 1]
        xs = p_ref[bi, :, width:2 * width]
        xd = xs * dt_x
        acs = small[bi][1]
        pre.append(dict(acs=acs, acs_t=acs.T, xs=xs, xd=xd, e_acs=jnp.exp(acs_x),
                        xdec=xd * jnp.exp(atot_x - acs_x), p_end=jnp.exp(atot_x)))

    t_idx = lax.broadcasted_iota(jnp.int32, (q, q), 0)
    s_idx = lax.broadcasted_iota(jnp.int32, (q, q), 1)
    incl = (s_idx >= t_idx) if reverse else (s_idx <= t_idx)
    lane = lax.broadcasted_iota(jnp.int32, (1, PAIR), 1)
    m0 = lane < HEAD

    groups = [(bi, g) for bi in range(nb) for g in range(n_groups)]
    cms, bmts, cbs = [], [], []
    for bi, g in groups:
        b0 = 2 * width + g * n_state
        c0 = 2 * width + (n_groups + g) * n_state
        bm = p_ref[bi, :, b0:b0 + n_state]
        cm_b = p_ref[bi, :, c0:c0 + n_state].astype(BF16)
        cms.append(cm_b)
        cbs.append(_dot_nt(cm_b, bm.astype(BF16)))
        bmts.append(bm.T.astype(BF16))
    units = [(bi, j) for bi in range(nb) for j in range(n_pairs)]
    ys = []
    for u, (bi, j) in enumerate(units):
        pb = pre[bi]
        gi = bi * n_groups + j // pairs_per_group
        sl = slice(j * PAIR, (j + 1) * PAIR)
        ms = []
        for hh in range(2):
            li = d * heads + 2 * j + hh
            diff = pb["acs"][:, li:li + 1] - pb["acs_t"][li:li + 1, :]
            ms.append((cbs[gi] * jnp.exp(jnp.where(incl, diff, NEG_BIG))).astype(BF16))
        y_diag = _dot(jnp.concatenate(ms, axis=1), _stack2(pb["xd"][:, sl], m0).astype(BF16))
        y_off = _dot(cms[gi], st_ref[u].astype(BF16)) * pb["e_acs"][:, sl]
        ys.append(y_diag + y_off)
    for u, (bi, j) in enumerate(units):
        pb = pre[bi]
        gi = bi * n_groups + j // pairs_per_group
        sl = slice(j * PAIR, (j + 1) * PAIR)
        st_ref[u] = st_ref[u] * pb["p_end"][:, sl] + _dot(bmts[gi], pb["xdec"][:, sl].astype(BF16))

    for bi in range(nb):
        y_all = jnp.concatenate(ys[bi * n_pairs:(bi + 1) * n_pairs], axis=1)
        gate = _silu(p_ref[bi, :, 0:width])
        if reverse:
            y_ref[bi] = y_all * gate
        else:
            y = (y_all + pre[bi]["xs"] * dsk_ref[...]) * gate + yb_ref[bi]
            ms = jnp.mean(y * y, axis=-1, keepdims=True)
            y_ref[bi] = (y * lax.rsqrt(ms + GATED_EPS) * nw_ref[...]).astype(y_ref.dtype)


def _ssd_scan(reverse, p_ssm, consts, nb, yb=None):
    b, t, ncol = p_ssm.shape
    width = consts["width"]
    n_state = consts["n_state"]
    q = SSD_CHUNK
    n_chunks = t // q

    def cidx(j):
        return (n_chunks - 1 - j) if reverse else j

    def row(x):
        return pl.BlockSpec((1, x.shape[1]), lambda i, j: (0, 0))

    def full(x):
        return pl.BlockSpec(x.shape, lambda i, j: (0,) * x.ndim)

    expand = consts["expand"][1 if reverse else 0]
    args = [p_ssm, consts["dt_bias"], consts["a_log"], expand]
    specs = [
        pl.BlockSpec((nb, q, ncol), lambda i, j: (i, cidx(j), 0)),
        row(consts["dt_bias"]), row(consts["a_log"]), full(expand),
    ]
    if not reverse:
        args += [consts["d_skip"], consts["norm_w"], yb]
        specs += [row(consts["d_skip"]), row(consts["norm_w"]),
                  pl.BlockSpec((nb, q, width), lambda i, j: (i, cidx(j), 0))]
    out_dtype = F32 if reverse else BF16
    return pl.pallas_call(
        functools.partial(_ssd_kernel, reverse, nb, width, n_state),
        grid=(b // nb, n_chunks),
        in_specs=specs,
        out_specs=pl.BlockSpec((nb, q, width), lambda i, j: (i, cidx(j), 0)),
        out_shape=jax.ShapeDtypeStruct((b, t, width), out_dtype),
        scratch_shapes=[pltpu.VMEM((nb * (width // PAIR), n_state, PAIR), F32)],
        compiler_params=_cparams(("arbitrary", "arbitrary")),
        name="ssd_bwd" if reverse else "ssd_fwd",
    )(*args)


def _outproj_kernel(half, x_ref, yr_ref, ys_ref, mod_ref, nw_ref, w_ref, x1_ref, h2_ref):
    attn = _dot(yr_ref[0], w_ref[:half, :]) + _dot(ys_ref[0], w_ref[half:, :])
    x1 = x_ref[0] + mod_ref[0, 2:3, :] * attn
    x1_ref[0] = x1
    h2_ref[0] = _rms_mod(x1, nw_ref[...], mod_ref[0, 4:5, :], mod_ref[0, 3:4, :]).astype(BF16)


def _out_proj(x, y_rw, y_ssm, mod, norm_w, w_out, tm):
    b, t, d = x.shape
    half = y_rw.shape[2]
    return pl.pallas_call(
        functools.partial(_outproj_kernel, half),
        grid=(b, t // tm),
        in_specs=[
            pl.BlockSpec((1, tm, d), lambda i, j: (i, j, 0)),
            pl.BlockSpec((1, tm, half), lambda i, j: (i, j, 0)),
            pl.BlockSpec((1, tm, half), lambda i, j: (i, j, 0)),
            pl.BlockSpec((1, 6, d), lambda i, j: (i, 0, 0)),
            pl.BlockSpec((1, d), lambda i, j: (0, 0)),
            pl.BlockSpec(memory_space=pltpu.VMEM),
        ],
        out_specs=[
            pl.BlockSpec((1, tm, d), lambda i, j: (i, j, 0)),
            pl.BlockSpec((1, tm, d), lambda i, j: (i, j, 0)),
        ],
        out_shape=[
            jax.ShapeDtypeStruct((b, t, d), F32),
            jax.ShapeDtypeStruct((b, t, d), BF16),
        ],
        compiler_params=_cparams(("arbitrary", "arbitrary")),
        name="out_proj",
    )(x, y_rw, y_ssm, mod, norm_w.reshape(1, d), w_out)


def _ffn_kernel(tm, n_tiles, n_ff, h_hbm, hp_ref, hn_ref, x1_hbm, mod_ref, modf_ref, fw_ref,
                wv_ref, wg_ref, cwv_ref, cwg_ref, cbv_ref, cbg_ref, wd_ref,
                o_ref, lhs_ref, uv_ref, ug_ref, x1_ref, x1_sem, h_sem):
    i = pl.program_id(1)
    f = pl.program_id(2)
    hal = SUBLANES_BF16
    rows_hbm = (pl.program_id(0), pl.ds(i * tm, tm))
    x1_copy = pltpu.make_async_copy(x1_hbm.at[rows_hbm], x1_ref, x1_sem)
    h_copy = pltpu.make_async_copy(h_hbm.at[rows_hbm], lhs_ref.at[pl.ds(hal, tm)], h_sem)

    @pl.when(f == 0)
    def _():
        h_copy.start()
        x1_copy.start()
        lhs_ref[0:hal, :] = jnp.where(i > 0, hp_ref[0], jnp.zeros_like(hp_ref[0]))
        lhs_ref[hal + tm:, :] = jnp.where(i < n_tiles - 1, hn_ref[0], jnp.zeros_like(hn_ref[0]))
        o_ref[0] = jnp.zeros(o_ref.shape[1:], F32)
        h_copy.wait()

    lhs = lhs_ref[...]
    ck = wv_ref.shape[1]
    subs = [(c, c + FFN_SUB) for c in range(0, ck, FFN_SUB)]
    for s, (c0, c1) in enumerate(subs):
        uv_ref[s] = _dot(lhs, wv_ref[:, c0:c1])
    for s, (c0, c1) in enumerate(subs):
        ug_ref[s] = _dot(lhs, wg_ref[:, c0:c1])

    def conv(u_ref, s, cw_ref, cb_ref, c0, c1):
        out = cb_ref[:, c0:c1] + u_ref[s, pl.ds(hal - 1, tm), :] * cw_ref[0:1, c0:c1]
        out = out + u_ref[s, pl.ds(hal, tm), :] * cw_ref[1:2, c0:c1]
        return out + u_ref[s, pl.ds(hal + 1, tm), :] * cw_ref[2:3, c0:c1]

    for s, (c0, c1) in enumerate(subs):
        act = _silu(conv(uv_ref, s, cwv_ref, cbv_ref, c0, c1)) * conv(ug_ref, s, cwg_ref, cbg_ref, c0, c1)
        o_ref[0] += _dot(act.astype(BF16), wd_ref[c0:c1, :])

    @pl.when(f == n_ff - 1)
    def _():
        x1_copy.wait()
        x2 = x1_ref[...] + mod_ref[0, 5:6, :] * o_ref[0]
        o_ref[0] = _rms_mod(x2, fw_ref[...], modf_ref[0, 1:2, :], modf_ref[0, 0:1, :])


def _ffn(h2, x1, mod, modf, final_w, w_up, conv_w, conv_b, w_down, tm, ck):
    b, t, d = x1.shape
    d_ff = w_down.shape[0]
    n_ff = d_ff // ck
    n_tiles = t // tm
    hal = SUBLANES_BF16
    hb = tm // hal
    last_hb = t // hal - 1
    cb2 = conv_b.reshape(1, 2 * d_ff)
    return pl.pallas_call(
        functools.partial(_ffn_kernel, tm, n_tiles, n_ff),
        grid=(b, n_tiles, n_ff),
        in_specs=[
            pl.BlockSpec(memory_space=pl.ANY),
            pl.BlockSpec((1, hal, d), lambda i, j, f: (i, jnp.maximum(j * hb - 1, 0), 0)),
            pl.BlockSpec((1, hal, d), lambda i, j, f: (i, jnp.minimum((j + 1) * hb, last_hb), 0)),
            pl.BlockSpec(memory_space=pl.ANY),
            pl.BlockSpec((1, 6, d), lambda i, j, f: (i, 0, 0)),
            pl.BlockSpec((1, 2, d), lambda i, j, f: (i, 0, 0)),
            pl.BlockSpec((1, d), lambda i, j, f: (0, 0)),
            pl.BlockSpec((d, ck), lambda i, j, f: (0, f)),
            pl.BlockSpec((d, ck), lambda i, j, f: (0, f + n_ff)),
            pl.BlockSpec((conv_w.shape[0], ck), lambda i, j, f: (0, f)),
            pl.BlockSpec((conv_w.shape[0], ck), lambda i, j, f: (0, f + n_ff)),
            pl.BlockSpec((1, ck), lambda i, j, f: (0, f)),
            pl.BlockSpec((1, ck), lambda i, j, f: (0, f + n_ff)),
            pl.BlockSpec((ck, d), lambda i, j, f: (f, 0)),
        ],
        out_specs=pl.BlockSpec((1, tm, d), lambda i, j, f: (i, j, 0)),
        out_shape=jax.ShapeDtypeStruct((b, t, d), F32),
        scratch_shapes=[
            pltpu.VMEM((tm + 2 * hal, d), BF16),
            pltpu.VMEM((ck // FFN_SUB, tm + 2 * hal, FFN_SUB), F32),
            pltpu.VMEM((ck // FFN_SUB, tm + 2 * hal, FFN_SUB), F32),
            pltpu.VMEM((tm, d), F32),
            pltpu.SemaphoreType.DMA(()),
            pltpu.SemaphoreType.DMA(()),
        ],
        compiler_params=_cparams(("arbitrary", "arbitrary", "arbitrary")),
        name="conv_glu_ffn",
    )(h2, h2, h2, x1, mod, modf, final_w.reshape(1, d), w_up, w_up, conv_w, conv_w, cb2, cb2, w_down)


def _pad_rows(x, rows):
    return jnp.concatenate([x, jnp.zeros((rows - x.shape[0],) + x.shape[1:], x.dtype)], axis=0)


def _pad_lanes(x, lanes):
    return jnp.concatenate([x, jnp.zeros(x.shape[:-1] + (lanes - x.shape[-1],), x.dtype)], axis=-1)


def _encoder(x, mod, modf, wts):
    rw, ssm = wts["rwkv"], wts["ssm"]
    xs_rw, p_ssm = _in_proj(x, mod, wts["norm1_w"], wts["w_in"], rw["mu"], ssm["conv_w"],
                            ssm["conv_b"], rw["ncol"], rw["width"], tm=256)
    nb = math.gcd(x.shape[0], SCAN_ROWS)
    yb = _rwkv_scan(True, xs_rw, rw, nb)
    y_rw = _rwkv_scan(False, xs_rw, rw, nb, yb)
    sb = _ssd_scan(True, p_ssm, ssm, nb)
    y_ssm = _ssd_scan(False, p_ssm, ssm, nb, sb)
    x1, h2 = _out_proj(x, y_rw, y_ssm, mod, wts["norm2_w"], wts["w_out"], tm=512)
    return _ffn(h2, x1, mod, modf, wts["final_norm_w"], wts["w_up"], wts["ffn_conv_w"],
                wts["ffn_conv_b"], wts["w_down"], tm=1024, ck=512)


def kernel(x_prompt, x_sample, c_prompt, c_sample, norm1_w, w_in, rwkv_mu, rwkv_w0, rwkv_w2, rwkv_a0, rwkv_a2, rwkv_g2, rwkv_k_k, rwkv_k_a, rwkv_r_k, rwkv_lnx_w, rwkv_lnx_b, ssm_conv_w, ssm_conv_b, ssm_dt_bias, ssm_a_log, ssm_d, ssm_norm_w, w_out, norm2_w, ffn_w_up, ffn_conv_w, ffn_conv_b, ffn_w_down, w_ada, b_ada, final_norm_w, w_ada_final, b_ada_final):
    assert w_in.shape[0] == 1, "single-layer trunk"
    d = x_prompt.shape[-1]
    width = rwkv_w0.shape[-1]
    heads = width // HEAD
    lora = rwkv_w2.shape[2]
    n_rw = rwkv_mu.shape[-1]
    n_state = (ssm_conv_w.shape[-1] - width) // 4
    assert 2 * lora == LANES and rwkv_g2.shape[1] == LANES and 2 * heads <= LANES

    n_ssm = w_in.shape[2] - n_rw
    n_ssm_pad = -(-n_ssm // LANES) * LANES
    w_cat = _pad_lanes(w_in[0].astype(BF16), n_rw + n_ssm_pad)
    zl = jnp.zeros((lora, width), F32)
    eye_h = jnp.repeat(jnp.eye(heads, dtype=F32), HEAD, axis=1)
    seg = jnp.arange(MXU_DIM) // HEAD
    rwkv = {
        "width": width, "ncol": n_rw,
        "mu": rwkv_mu[0],
        "w0": [rwkv_w0[0, i:i + 1] for i in range(2)],
        "a0": [rwkv_a0[0, i:i + 1] for i in range(2)],
        "w2": [jnp.concatenate([rwkv_w2[0, 0], zl], 0).astype(BF16),
               jnp.concatenate([zl, rwkv_w2[0, 1]], 0).astype(BF16)],
        "a2": [jnp.concatenate([rwkv_a2[0, 0], zl], 0).astype(BF16),
               jnp.concatenate([zl, rwkv_a2[0, 1]], 0).astype(BF16)],
        "g2": rwkv_g2[0].astype(BF16),
        "k_k": rwkv_k_k, "k_a": rwkv_k_a, "r_k": rwkv_r_k[0].reshape(1, width),
        "lnx_w": rwkv_lnx_w, "lnx_b": rwkv_lnx_b,
        "bd": (seg[:, None] == seg[None, :]).astype(BF16),
    }
    ssm = {
        "width": width, "n_state": n_state,
        "conv_w": ssm_conv_w[0], "conv_b": ssm_conv_b,
        "dt_bias": _pad_lanes(ssm_dt_bias[0].reshape(1, 2 * heads), LANES),
        "a_log": _pad_lanes(ssm_a_log[0].reshape(1, 2 * heads), LANES),
        "d_skip": jnp.repeat(ssm_d[0], HEAD).reshape(1, width),
        "norm_w": ssm_norm_w,
        "expand": [_pad_rows(jnp.concatenate([eye_h * (1 - i), eye_h * i], 0), LANES).astype(BF16)
                   for i in range(2)],
    }
    wts = {
        "norm1_w": norm1_w[0], "norm2_w": norm2_w[0], "final_norm_w": final_norm_w,
        "w_in": w_cat, "w_out": w_out[0].astype(BF16),
        "w_up": ffn_w_up[0].astype(BF16), "w_down": ffn_w_down[0].astype(BF16),
        "ffn_conv_w": ffn_conv_w[0], "ffn_conv_b": ffn_conv_b[0],
        "rwkv": rwkv, "ssm": ssm,
    }

    nb_p, nb_s = c_prompt.shape[0], c_sample.shape[0]
    rows = -(-(nb_p + nb_s) // 8) * 8
    c_all = _pad_rows(jnp.concatenate([c_prompt, c_sample], 0), rows)
    mod_all = _ada(c_all, w_ada[0], b_ada[0]).reshape(rows, 6, d)
    modf_all = _ada(c_all, w_ada_final, b_ada_final).reshape(rows, 2, d)

    outs = []
    for x, lo, hi in ((x_prompt, 0, nb_p), (x_sample, nb_p, nb_p + nb_s)):
        outs.append(_encoder(x, mod_all[lo:hi], modf_all[lo:hi], wts))
    return tuple(outs)
```

```python
import functools
import math

import jax
import jax.numpy as jnp
from jax import lax
from jax.experimental import pallas as pl
from jax.experimental.pallas import tpu as pltpu

F32 = jnp.float32
BF16 = jnp.bfloat16

LANES = 128
SUBLANES_BF16 = 16
MXU_DIM = 256
VMEM_LIMIT_BYTES = 60 * 1024 * 1024

NORM_EPS = 1e-6
GN_EPS = 64e-5
GATED_EPS = 1e-5

HEAD = 64
PAIR = 2 * HEAD
RWKV_CHUNK = 64
SSD_CHUNK = 128
SCAN_ROWS = 4
HALO_ROWS = 8
INPROJ_BLOCK = MXU_DIM
FFN_SUB = MXU_DIM
NEG_BIG = -1e30


def _cparams(sem):
    return pltpu.CompilerParams(dimension_semantics=sem, vmem_limit_bytes=VMEM_LIMIT_BYTES)


def _dot(a, b):
    return jnp.dot(a, b, preferred_element_type=F32)


def _dot_nt(a, b):
    return lax.dot_general(a, b, (((1,), (1,)), ((), ())), preferred_element_type=F32)


def _sigmoid(x):
    return 1.0 / (1.0 + jnp.exp(-x))


def _silu(x):
    return x * _sigmoid(x)


def _split3(x):
    hi = x.astype(BF16)
    r1 = x - hi.astype(F32)
    mid = r1.astype(BF16)
    lo = (r1 - mid.astype(F32)).astype(BF16)
    return hi, mid, lo


def _cumsum_rows(x, tri3, n):
    hi, mid, lo = _split3(x)
    stacked = jnp.concatenate([hi, mid, lo], axis=0)
    out = _dot(tri3, stacked)
    return out[:n], out[n:n + 1]


def _make_tri3(n, reverse):
    rows = lax.broadcasted_iota(jnp.int32, (n + 8, 3 * n), 0)
    cols = lax.broadcasted_iota(jnp.int32, (n + 8, 3 * n), 1) & (n - 1)
    tri = (cols >= rows) if reverse else (cols <= rows)
    return jnp.where(tri | (rows >= n), 1.0, 0.0).astype(BF16)


def _segsum_heads(xs, bd):
    n = xs[0].shape[0]
    parts = []
    for x in xs:
        hi = x.astype(BF16)
        parts += [hi, (x - hi.astype(F32)).astype(BF16)]
    stacked = jnp.concatenate(parts, axis=0)
    out = jnp.concatenate(
        [_dot(stacked[:, i * MXU_DIM:(i + 1) * MXU_DIM], bd)
         for i in range(stacked.shape[1] // MXU_DIM)], axis=1)
    return [out[2 * i * n:(2 * i + 1) * n] + out[(2 * i + 1) * n:(2 * i + 2) * n]
            for i in range(len(xs))]


def _rms_mod(x, w, scale, shift):
    ms = jnp.mean(x * x, axis=-1, keepdims=True)
    return x * lax.rsqrt(ms + NORM_EPS) * (w * (1.0 + scale)) + shift


def _ada_kernel(c_ref, w_ref, b_ref, o_ref):
    c = c_ref[...]
    o_ref[...] = _dot(_silu(c), w_ref[...]) + b_ref[...]


def _ada(c_pad, w, b):
    rows, d = c_pad.shape
    n = w.shape[1]
    tn = 1024
    return pl.pallas_call(
        _ada_kernel,
        grid=(n // tn,),
        in_specs=[
            pl.BlockSpec((rows, d), lambda j: (0, 0)),
            pl.BlockSpec((d, tn), lambda j: (0, j)),
            pl.BlockSpec((1, tn), lambda j: (0, j)),
        ],
        out_specs=pl.BlockSpec((rows, tn), lambda j: (0, j)),
        out_shape=jax.ShapeDtypeStruct((rows, n), F32),
        compiler_params=_cparams(("arbitrary",)),
        name="ada_mod",
    )(c_pad, w, b.reshape(1, n))


def _inproj_kernel(tm, n_tiles, n_rw, width, x_ref, xp_ref, xn_ref, mod_ref, nw_ref, w_ref, mu_ref,
                   cw_ref, cb_ref, orw_ref, ossm_ref):
    j = pl.program_id(1)
    nw, scale, shift = nw_ref[...], mod_ref[0, 1:2, :], mod_ref[0, 0:1, :]
    h_prev = jnp.where(j > 0, _rms_mod(xp_ref[0], nw, scale, shift), 0.0)
    h_next = jnp.where(j < n_tiles - 1, _rms_mod(xn_ref[0], nw, scale, shift), 0.0)
    h = jnp.concatenate([h_prev, _rms_mod(x_ref[0], nw, scale, shift), h_next], axis=0).astype(BF16)
    rows = tm + 2 * HALO_ROWS
    mid = slice(HALO_ROWS, HALO_ROWS + tm)

    conv_k, conv_ch = cw_ref.shape
    left = (conv_k - 1) // 2
    n_ssm = ossm_ref.shape[2]

    def blocks(lo, hi):
        return [(c, min(c + INPROJ_BLOCK, hi)) for c in range(lo, hi, INPROJ_BLOCK)]

    def shift_tail(p, c0, c1):
        p_prev = pltpu.roll(p, 1, 0)
        p_next = pltpu.roll(p, rows - 1, 0)
        xs = p + mu_ref[0:1, c0:c1] * (p_prev - p) + mu_ref[1:2, c0:c1] * (p_next - p)
        orw_ref[0, :, c0:c1] = xs[mid]

    def copy_tail(p, c0, c1):
        ossm_ref[0, :, c0:c1] = p[mid]

    def conv_tail(ext, c0, c1):
        acc = cb_ref[:, c0:c1] + ext[mid] * cw_ref[left:left + 1, c0:c1]
        for i in range(conv_k):
            if i != left:
                shifted = pltpu.roll(ext, (left - i) % rows, 0)
                acc = acc + shifted[mid] * cw_ref[i:i + 1, c0:c1]
        ossm_ref[0, :, width + c0:width + c1] = _silu(acc)

    work = [(shift_tail, 0, c0, c1) for c0, c1 in blocks(0, n_rw)]
    work += [(conv_tail, n_rw + width, c0, c1) for c0, c1 in blocks(0, conv_ch)]
    work += [(copy_tail, n_rw, c0, c1) for c0, c1 in blocks(0, width) + blocks(width + conv_ch, n_ssm)]
    for tail, base, c0, c1 in work:
        tail(_dot(h, w_ref[:, base + c0:base + c1]), c0, c1)


def _in_proj(x, mod, norm_w, w_cat, mu, conv_w, conv_b, n_rw, width, tm):
    b, t, d = x.shape
    n_ssm = w_cat.shape[1] - n_rw
    n_tiles = t // tm
    hb = tm // HALO_ROWS
    last_hb = t // HALO_ROWS - 1
    return pl.pallas_call(
        functools.partial(_inproj_kernel, tm, n_tiles, n_rw, width),
        grid=(b, n_tiles),
        in_specs=[
            pl.BlockSpec((1, tm, d), lambda i, j: (i, j, 0)),
            pl.BlockSpec((1, HALO_ROWS, d), lambda i, j: (i, jnp.maximum(j * hb - 1, 0), 0)),
            pl.BlockSpec((1, HALO_ROWS, d), lambda i, j: (i, jnp.minimum((j + 1) * hb, last_hb), 0)),
            pl.BlockSpec((1, 6, d), lambda i, j: (i, 0, 0)),
            pl.BlockSpec((1, d), lambda i, j: (0, 0)),
            pl.BlockSpec(memory_space=pltpu.VMEM),
            pl.BlockSpec(mu.shape, lambda i, j: (0, 0)),
            pl.BlockSpec(conv_w.shape, lambda i, j: (0, 0)),
            pl.BlockSpec(conv_b.shape, lambda i, j: (0, 0)),
        ],
        out_specs=[
            pl.BlockSpec((1, tm, n_rw), lambda i, j: (i, j, 0)),
            pl.BlockSpec((1, tm, n_ssm), lambda i, j: (i, j, 0)),
        ],
        out_shape=[
            jax.ShapeDtypeStruct((b, t, n_rw), F32),
            jax.ShapeDtypeStruct((b, t, n_ssm), F32),
        ],
        compiler_params=_cparams(("arbitrary", "arbitrary")),
        name="in_proj",
    )(x, x, x, mod, norm_w.reshape(1, d), w_cat, mu, conv_w, conv_b)


def _stack2(x, m0):
    return jnp.concatenate([jnp.where(m0, x, 0.0), jnp.where(m0, 0.0, x)], axis=0)


def _rwkv_kernel(reverse, nb, width, *refs):
    if reverse:
        (xs_ref, w0_ref, w2_ref, a0_ref, a2_ref, kk_ref, ka_ref, bd_ref, y_ref, st_ref) = refs
    else:
        (xs_ref, w0_ref, w2_ref, a0_ref, a2_ref, kk_ref, ka_ref, bd_ref,
         a0o_ref, a2o_ref, g2_ref, rk_ref, lw_ref, lb_ref, yb_ref, y_ref, st_ref) = refs
    n = RWKV_CHUNK
    n_pairs = width // PAIR

    @pl.when(pl.program_id(1) == 0)
    def _():
        st_ref[...] = jnp.zeros_like(st_ref)

    bd = bd_ref[...]
    k_a = ka_ref[...]
    tri3 = _make_tri3(n, reverse)
    o = 3 * width

    def cols(c0, c1):
        return xs_ref[:, :, c0:c1].reshape(nb * n, c1 - c0)

    r, k, v = cols(0, width), cols(width, 2 * width), cols(2 * width, o)
    ad_b = cols(o + LANES, o + 2 * LANES).astype(BF16)
    w_log = w0_ref[...] + _dot(jnp.tanh(cols(o, o + LANES)).astype(BF16), w2_ref[...])
    lw = -math.exp(-0.5) * _sigmoid(w_log)
    a = _sigmoid(a0_ref[...] + _dot(ad_b, a2_ref[...]))
    kk = k * kk_ref[...]
    k_d = k * (1.0 + (a - 1.0) * k_a)
    if reverse:
        (kk_sq,) = _segsum_heads([kk * kk], bd)
    else:
        a_o = _sigmoid(a0o_ref[...] + _dot(ad_b, a2o_ref[...]))
        gate = _dot(_sigmoid(cols(o + 2 * LANES, o + 3 * LANES)).astype(BF16), g2_ref[...])
        ksum = k_d + k * (1.0 + (a_o - 1.0) * k_a)
        kk_sq, rk_sum = _segsum_heads([kk * kk, r * ksum * rk_ref[...]], bd)
    kk = kk / jnp.maximum(jnp.sqrt(kk_sq), 1e-12)
    beta = kk * a
    cums = [_cumsum_rows(lw[bi * n:(bi + 1) * n], tri3, n) for bi in range(nb)]
    c = jnp.concatenate([cu[0] for cu in cums], axis=0)
    ctot_rows = jnp.concatenate([jnp.broadcast_to(cu[1], (n, width)) for cu in cums], axis=0)
    e_nc = jnp.exp(-c)
    e_end = jnp.exp(ctot_rows - c)
    pre = dict(v=v, a_t=-kk * jnp.exp(c - lw), r_t=r * jnp.exp(c), b_t=beta * e_nc, k_t=k_d * e_nc,
               b_h=beta * e_end, k_h=k_d * e_end)
    p_end = [jnp.exp(cu[1]) for cu in cums]

    lane = lax.broadcasted_iota(jnp.int32, (1, PAIR), 1)
    m0 = lane < HEAD
    t_idx = lax.broadcasted_iota(jnp.int32, (n, 2 * n), 0)
    col = lax.broadcasted_iota(jnp.int32, (n, 2 * n), 1)
    s_idx = col & (n - 1)
    col_h0 = col < n
    if reverse:
        strict, incl = s_idx > t_idx, s_idx >= t_idx
    else:
        strict, incl = s_idx < t_idx, s_idx <= t_idx
    incl2 = jnp.concatenate([incl, incl], axis=1)
    eye2 = jnp.where(s_idx == t_idx, 1.0, 0.0)
    br = lax.broadcasted_iota(jnp.int32, (PAIR, PAIR), 0) < HEAD
    bc = lax.broadcasted_iota(jnp.int32, (PAIR, PAIR), 1) < HEAD
    blk = br == bc

    def blockdiag(x):
        return jnp.concatenate(
            [jnp.where(col_h0, x, 0.0), jnp.where(col_h0, 0.0, x)], axis=0).astype(BF16)

    units = [(bi, j) for bi in range(nb) for j in range(n_pairs)]
    ids = range(len(units))

    def col_of(name, u):
        bi, j = units[u]
        return pre[name][bi * n:(bi + 1) * n, j * PAIR:(j + 1) * PAIR]

    g1s, g2s = [], []
    for u in ids:
        lhs = jnp.concatenate([col_of("a_t", u), col_of("r_t", u)], axis=0).astype(BF16)
        rhs = jnp.concatenate([_stack2(col_of("b_t", u), m0), _stack2(col_of("k_t", u), m0)],
                              axis=0).astype(BF16)
        g1s.append(_dot_nt(lhs, rhs))
        g2s.append(_dot_nt(lhs, st_ref[u].astype(BF16)))
    a_ab = [jnp.where(strict, g1s[u][:n, :2 * n], 0.0) for u in ids]
    n_fac = int(math.log2(n))
    tmat = [eye2 + a_ab[u] for u in ids]
    pw = [_dot(a_ab[u].astype(BF16), blockdiag(a_ab[u])) for u in ids]
    for _ in range(1, n_fac - 1):
        both = [_dot(jnp.concatenate([tmat[u], pw[u]], axis=0).astype(BF16), blockdiag(pw[u]))
                for u in ids]
        tmat = [tmat[u] + both[u][:n] for u in ids]
        pw = [both[u][n:] for u in ids]
    tmat = [tmat[u] + _dot(tmat[u].astype(BF16), blockdiag(pw[u])) for u in ids]
    v2 = [_stack2(col_of("v", u), m0).astype(BF16) for u in ids]
    w_rhs = [g2s[u][:n] + _dot(jnp.where(strict, g1s[u][:n, 2 * n:], 0.0).astype(BF16), v2[u])
             for u in ids]
    us = [_dot(tmat[u].astype(BF16), _stack2(w_rhs[u], m0).astype(BF16)) for u in ids]
    ys = []
    for u in ids:
        a_r = jnp.where(incl2, g1s[u][n:], 0.0).astype(BF16)
        uv2 = jnp.concatenate([_stack2(us[u], m0).astype(BF16), v2[u]], axis=0)
        ys.append(g2s[u][n:] + _dot(a_r, uv2))
    for u in ids:
        uv_t = jnp.concatenate([us[u], col_of("v", u)], axis=0).T.astype(BF16)
        bk = jnp.concatenate([col_of("b_h", u), col_of("k_h", u)], axis=0).astype(BF16)
        bi, j = units[u]
        st_ref[u] = (st_ref[u] * p_end[bi][:, j * PAIR:(j + 1) * PAIR]
                     + jnp.where(blk, _dot(uv_t, bk), 0.0))

    y_all = jnp.concatenate(
        [jnp.concatenate(ys[bi * n_pairs:(bi + 1) * n_pairs], axis=1) for bi in range(nb)], axis=0)
    if reverse:
        y_ref[...] = y_all.reshape(nb, n, width)
    else:
        ysum = y_all + yb_ref[...].reshape(nb * n, width)
        inv = 1.0 / HEAD
        (mean,) = _segsum_heads([ysum], bd)
        dlt = ysum - mean * inv
        (var,) = _segsum_heads([dlt * dlt], bd)
        yn = dlt * lax.rsqrt(var * inv + GN_EPS) * lw_ref[...] + lb_ref[...]
        y_ref[...] = ((yn + rk_sum * v) * gate).astype(y_ref.dtype).reshape(nb, n, width)


def _rwkv_scan(reverse, xs, consts, nb, yb=None):
    b, t, ncol = xs.shape
    width = consts["width"]
    n = RWKV_CHUNK
    n_chunks = t // n

    def cidx(j):
        return (n_chunks - 1 - j) if reverse else j

    def row(x):
        return pl.BlockSpec((1, x.shape[1]), lambda i, j: (0, 0))

    def full(x):
        return pl.BlockSpec(x.shape, lambda i, j: (0,) * x.ndim)

    d = 1 if reverse else 0
    args = [xs, consts["w0"][d], consts["w2"][d], consts["a0"][d], consts["a2"][d], consts["k_k"],
            consts["k_a"], consts["bd"]]
    specs = [
        pl.BlockSpec((nb, n, ncol), lambda i, j: (i, cidx(j), 0)),
        row(consts["w0"][d]), full(consts["w2"][d]), row(consts["a0"][d]),
        full(consts["a2"][d]), row(consts["k_k"]), row(consts["k_a"]), full(consts["bd"]),
    ]
    if not reverse:
        args += [consts["a0"][1], consts["a2"][1], consts["g2"], consts["r_k"], consts["lnx_w"],
                 consts["lnx_b"], yb]
        specs += [row(consts["a0"][1]), full(consts["a2"][1]), full(consts["g2"]), row(consts["r_k"]),
                  row(consts["lnx_w"]), row(consts["lnx_b"]),
                  pl.BlockSpec((nb, n, width), lambda i, j: (i, cidx(j), 0))]
    out_dtype = F32 if reverse else BF16
    return pl.pallas_call(
        functools.partial(_rwkv_kernel, reverse, nb, width),
        grid=(b // nb, n_chunks),
        in_specs=specs,
        out_specs=pl.BlockSpec((nb, n, width), lambda i, j: (i, cidx(j), 0)),
        out_shape=jax.ShapeDtypeStruct((b, t, width), out_dtype),
        scratch_shapes=[pltpu.VMEM((nb * (width // PAIR), PAIR, PAIR), F32)],
        compiler_params=_cparams(("arbitrary", "arbitrary")),
        name="rwkv_bwd" if reverse else "rwkv_fwd",
    )(*args)


def _ssd_kernel(reverse, nb, width, n_state, *refs):
    if reverse:
        (p_ref, dtb_ref, alog_ref, ed_ref, y_ref, st_ref) = refs
    else:
        (p_ref, dtb_ref, alog_ref, ed_ref, dsk_ref, nw_ref, yb_ref, y_ref, st_ref) = refs
    q = SSD_CHUNK
    n_groups = 2
    conv_ch = width + 2 * n_groups * n_state
    c1 = width + conv_ch
    heads = width // HEAD
    n_pairs = heads // 2
    pairs_per_group = n_pairs // n_groups
    d = 1 if reverse else 0

    @pl.when(pl.program_id(1) == 0)
    def _():
        st_ref[...] = jnp.zeros_like(st_ref)

    tri3 = _make_tri3(q, reverse)
    ed = ed_ref[...]
    neg_a = -jnp.exp(alog_ref[...])

    small = []
    for bi in range(nb):
        z_dt = p_ref[bi, :, c1:c1 + LANES] + dtb_ref[...]
        dt = jnp.maximum(z_dt, 0.0) + jnp.log(1.0 + jnp.exp(-jnp.abs(z_dt)))
        acs, atot = _cumsum_rows(dt * neg_a, tri3, q)
        small.append((dt, acs, jnp.broadcast_to(atot, (HALO_ROWS, LANES))))
    rows_bi = 2 * q + HALO_ROWS
    terms = _split3(jnp.concatenate([x for s in small for x in s], axis=0))
    ex = _dot(jnp.concatenate(terms, axis=0), ed)
    n_rows = nb * rows_bi
    ex = ex[:n_rows] + ex[n_rows:2 * n_rows] + ex[2 * n_rows:]

    pre = []
    for bi in range(nb):
        base = bi * rows_bi
        dt_x, acs_x, atot_x = ex[base:base + q], ex[base + q:base + 2 * q], ex[base + 2 * q:base + 2 * q + 1]
        xs = p_ref[bi, :, width:2 * width]
        xd = xs * dt_x
        acs = small[bi][1]
        pre.append(dict(acs=acs, acs_t=acs.T, xs=xs, xd=xd, e_acs=jnp.exp(acs_x),
                        xdec=xd * jnp.exp(atot_x - acs_x), p_end=jnp.exp(atot_x)))

    t_idx = lax.broadcasted_iota(jnp.int32, (q, q), 0)
    s_idx = lax.broadcasted_iota(jnp.int32, (q, q), 1)
    incl = (s_idx >= t_idx) if reverse else (s_idx <= t_idx)
    lane = lax.broadcasted_iota(jnp.int32, (1, PAIR), 1)
    m0 = lane < HEAD

    groups = [(bi, g) for bi in range(nb) for g in range(n_groups)]
    cms, bmts, cbs = [], [], []
    for bi, g in groups:
        b0 = 2 * width + g * n_state
        c0 = 2 * width + (n_groups + g) * n_state
        bm = p_ref[bi, :, b0:b0 + n_state]
        cm_b = p_ref[bi, :, c0:c0 + n_state].astype(BF16)
        cms.append(cm_b)
        cbs.append(_dot_nt(cm_b, bm.astype(BF16)))
        bmts.append(bm.T.astype(BF16))
    units = [(bi, j) for bi in range(nb) for j in range(n_pairs)]
    ys = []
    for u, (bi, j) in enumerate(units):
        pb = pre[bi]
        gi = bi * n_groups + j // pairs_per_group
        sl = slice(j * PAIR, (j + 1) * PAIR)
        ms = []
        for hh in range(2):
            li = d * heads + 2 * j + hh
            diff = pb["acs"][:, li:li + 1] - pb["acs_t"][li:li + 1, :]
            ms.append((cbs[gi] * jnp.exp(jnp.where(incl, diff, NEG_BIG))).astype(BF16))
        y_diag = _dot(jnp.concatenate(ms, axis=1), _stack2(pb["xd"][:, sl], m0).astype(BF16))
        y_off = _dot(cms[gi], st_ref[u].astype(BF16)) * pb["e_acs"][:, sl]
        ys.append(y_diag + y_off)
    for u, (bi, j) in enumerate(units):
        pb = pre[bi]
        gi = bi * n_groups + j // pairs_per_group
        sl = slice(j * PAIR, (j + 1) * PAIR)
        st_ref[u] = st_ref[u] * pb["p_end"][:, sl] + _dot(bmts[gi], pb["xdec"][:, sl].astype(BF16))

    for bi in range(nb):
        y_all = jnp.concatenate(ys[bi * n_pairs:(bi + 1) * n_pairs], axis=1)
        gate = _silu(p_ref[bi, :, 0:width])
        if reverse:
            y_ref[bi] = y_all * gate
        else:
            y = (y_all + pre[bi]["xs"] * dsk_ref[...]) * gate + yb_ref[bi]
            ms = jnp.mean(y * y, axis=-1, keepdims=True)
            y_ref[bi] = (y * lax.rsqrt(ms + GATED_EPS) * nw_ref[...]).astype(y_ref.dtype)


def _ssd_scan(reverse, p_ssm, consts, nb, yb=None):
    b, t, ncol = p_ssm.shape
    width = consts["width"]
    n_state = consts["n_state"]
    q = SSD_CHUNK
    n_chunks = t // q

    def cidx(j):
        return (n_chunks - 1 - j) if reverse else j

    def row(x):
        return pl.BlockSpec((1, x.shape[1]), lambda i, j: (0, 0))

    def full(x):
        return pl.BlockSpec(x.shape, lambda i, j: (0,) * x.ndim)

    expand = consts["expand"][1 if reverse else 0]
    args = [p_ssm, consts["dt_bias"], consts["a_log"], expand]
    specs = [
        pl.BlockSpec((nb, q, ncol), lambda i, j: (i, cidx(j), 0)),
        row(consts["dt_bias"]), row(consts["a_log"]), full(expand),
    ]
    if not reverse:
        args += [consts["d_skip"], consts["norm_w"], yb]
        specs += [row(consts["d_skip"]), row(consts["norm_w"]),
                  pl.BlockSpec((nb, q, width), lambda i, j: (i, cidx(j), 0))]
    out_dtype = F32 if reverse else BF16
    return pl.pallas_call(
        functools.partial(_ssd_kernel, reverse, nb, width, n_state),
        grid=(b // nb, n_chunks),
        in_specs=specs,
        out_specs=pl.BlockSpec((nb, q, width), lambda i, j: (i, cidx(j), 0)),
        out_shape=jax.ShapeDtypeStruct((b, t, width), out_dtype),
        scratch_shapes=[pltpu.VMEM((nb * (width // PAIR), n_state, PAIR), F32)],
        compiler_params=_cparams(("arbitrary", "arbitrary")),
        name="ssd_bwd" if reverse else "ssd_fwd",
    )(*args)


def _outproj_kernel(half, x_ref, yr_ref, ys_ref, mod_ref, nw_ref, w_ref, x1_ref, h2_ref):
    attn = _dot(yr_ref[0], w_ref[:half, :]) + _dot(ys_ref[0], w_ref[half:, :])
    x1 = x_ref[0] + mod_ref[0, 2:3, :] * attn
    x1_ref[0] = x1
    h2_ref[0] = _rms_mod(x1, nw_ref[...], mod_ref[0, 4:5, :], mod_ref[0, 3:4, :]).astype(BF16)


def _out_proj(x, y_rw, y_ssm, mod, norm_w, w_out, tm):
    b, t, d = x.shape
    half = y_rw.shape[2]
    return pl.pallas_call(
        functools.partial(_outproj_kernel, half),
        grid=(b, t // tm),
        in_specs=[
            pl.BlockSpec((1, tm, d), lambda i, j: (i, j, 0)),
            pl.BlockSpec((1, tm, half), lambda i, j: (i, j, 0)),
            pl.BlockSpec((1, tm, half), lambda i, j: (i, j, 0)),
            pl.BlockSpec((1, 6, d), lambda i, j: (i, 0, 0)),
            pl.BlockSpec((1, d), lambda i, j: (0, 0)),
            pl.BlockSpec(memory_space=pltpu.VMEM),
        ],
        out_specs=[
            pl.BlockSpec((1, tm, d), lambda i, j: (i, j, 0)),
            pl.BlockSpec((1, tm, d), lambda i, j: (i, j, 0)),
        ],
        out_shape=[
            jax.ShapeDtypeStruct((b, t, d), F32),
            jax.ShapeDtypeStruct((b, t, d), BF16),
        ],
        compiler_params=_cparams(("arbitrary", "arbitrary")),
        name="out_proj",
    )(x, y_rw, y_ssm, mod, norm_w.reshape(1, d), w_out)


def _ffn_kernel(tm, n_tiles, n_ff, h_ref, hp_ref, hn_ref, x1_hbm, mod_ref, modf_ref, fw_ref,
                wv_ref, wg_ref, cwv_ref, cwg_ref, cbv_ref, cbg_ref, wd_ref,
                o_ref, lhs_ref, uv_ref, ug_ref, x1_ref, x1_sem):
    i = pl.program_id(1)
    f = pl.program_id(2)
    hal = SUBLANES_BF16
    x1_copy = pltpu.make_async_copy(
        x1_hbm.at[pl.program_id(0), pl.ds(i * tm, tm), :], x1_ref, x1_sem)

    @pl.when(f == 0)
    def _():
        x1_copy.start()
        lhs_ref[0:hal, :] = jnp.where(i > 0, hp_ref[0], jnp.zeros_like(hp_ref[0]))
        lhs_ref[hal:hal + tm, :] = h_ref[0]
        lhs_ref[hal + tm:, :] = jnp.where(i < n_tiles - 1, hn_ref[0], jnp.zeros_like(hn_ref[0]))
        o_ref[0] = jnp.zeros(o_ref.shape[1:], F32)

    lhs = lhs_ref[...]
    ck = wv_ref.shape[1]
    subs = [(c, c + FFN_SUB) for c in range(0, ck, FFN_SUB)]
    for s, (c0, c1) in enumerate(subs):
        uv_ref[s] = _dot(lhs, wv_ref[:, c0:c1])
    for s, (c0, c1) in enumerate(subs):
        ug_ref[s] = _dot(lhs, wg_ref[:, c0:c1])

    def conv(u_ref, s, cw_ref, cb_ref, c0, c1):
        out = cb_ref[:, c0:c1] + u_ref[s, pl.ds(hal - 1, tm), :] * cw_ref[0:1, c0:c1]
        out = out + u_ref[s, pl.ds(hal, tm), :] * cw_ref[1:2, c0:c1]
        return out + u_ref[s, pl.ds(hal + 1, tm), :] * cw_ref[2:3, c0:c1]

    for s, (c0, c1) in enumerate(subs):
        act = _silu(conv(uv_ref, s, cwv_ref, cbv_ref, c0, c1)) * conv(ug_ref, s, cwg_ref, cbg_ref, c0, c1)
        o_ref[0] += _dot(act.astype(BF16), wd_ref[c0:c1, :])

    @pl.when(f == n_ff - 1)
    def _():
        x1_copy.wait()
        x2 = x1_ref[...] + mod_ref[0, 5:6, :] * o_ref[0]
        o_ref[0] = _rms_mod(x2, fw_ref[...], modf_ref[0, 1:2, :], modf_ref[0, 0:1, :])


def _ffn(h2, x1, mod, modf, final_w, w_up, conv_w, conv_b, w_down, tm, ck):
    b, t, d = x1.shape
    d_ff = w_down.shape[0]
    n_ff = d_ff // ck
    n_tiles = t // tm
    hal = SUBLANES_BF16
    hb = tm // hal
    last_hb = t // hal - 1
    cb2 = conv_b.reshape(1, 2 * d_ff)
    return pl.pallas_call(
        functools.partial(_ffn_kernel, tm, n_tiles, n_ff),
        grid=(b, n_tiles, n_ff),
        in_specs=[
            pl.BlockSpec((1, tm, d), lambda i, j, f: (i, j, 0)),
            pl.BlockSpec((1, hal, d), lambda i, j, f: (i, jnp.maximum(j * hb - 1, 0), 0)),
            pl.BlockSpec((1, hal, d), lambda i, j, f: (i, jnp.minimum((j + 1) * hb, last_hb), 0)),
            pl.BlockSpec(memory_space=pl.ANY),
            pl.BlockSpec((1, 6, d), lambda i, j, f: (i, 0, 0)),
            pl.BlockSpec((1, 2, d), lambda i, j, f: (i, 0, 0)),
            pl.BlockSpec((1, d), lambda i, j, f: (0, 0)),
            pl.BlockSpec((d, ck), lambda i, j, f: (0, f)),
            pl.BlockSpec((d, ck), lambda i, j, f: (0, f + n_ff)),
            pl.BlockSpec((conv_w.shape[0], ck), lambda i, j, f: (0, f)),
            pl.BlockSpec((conv_w.shape[0], ck), lambda i, j, f: (0, f + n_ff)),
            pl.BlockSpec((1, ck), lambda i, j, f: (0, f)),
            pl.BlockSpec((1, ck), lambda i, j, f: (0, f + n_ff)),
            pl.BlockSpec((ck, d), lambda i, j, f: (f, 0)),
        ],
        out_specs=pl.BlockSpec((1, tm, d), lambda i, j, f: (i, j, 0)),
        out_shape=jax.ShapeDtypeStruct((b, t, d), F32),
        scratch_shapes=[
            pltpu.VMEM((tm + 2 * hal, d), BF16),
            pltpu.VMEM((ck // FFN_SUB, tm + 2 * hal, FFN_SUB), F32),
            pltpu.VMEM((ck // FFN_SUB, tm + 2 * hal, FFN_SUB), F32),
            pltpu.VMEM((tm, d), F32),
            pltpu.SemaphoreType.DMA(()),
        ],
        compiler_params=_cparams(("arbitrary", "arbitrary", "arbitrary")),
        name="conv_glu_ffn",
    )(h2, h2, h2, x1, mod, modf, final_w.reshape(1, d), w_up, w_up, conv_w, conv_w, cb2, cb2, w_down)


def _pad_rows(x, rows):
    return jnp.concatenate([x, jnp.zeros((rows - x.shape[0],) + x.shape[1:], x.dtype)], axis=0)


def _pad_lanes(x, lanes):
    return jnp.concatenate([x, jnp.zeros(x.shape[:-1] + (lanes - x.shape[-1],), x.dtype)], axis=-1)


def _encoder(x, mod, modf, wts):
    rw, ssm = wts["rwkv"], wts["ssm"]
    xs_rw, p_ssm = _in_proj(x, mod, wts["norm1_w"], wts["w_in"], rw["mu"], ssm["conv_w"],
                            ssm["conv_b"], rw["ncol"], rw["width"], tm=256)
    nb = math.gcd(x.shape[0], SCAN_ROWS)
    yb = _rwkv_scan(True, xs_rw, rw, nb)
    y_rw = _rwkv_scan(False, xs_rw, rw, nb, yb)
    sb = _ssd_scan(True, p_ssm, ssm, nb)
    y_ssm = _ssd_scan(False, p_ssm, ssm, nb, sb)
    x1, h2 = _out_proj(x, y_rw, y_ssm, mod, wts["norm2_w"], wts["w_out"], tm=512)
    return _ffn(h2, x1, mod, modf, wts["final_norm_w"], wts["w_up"], wts["ffn_conv_w"],
                wts["ffn_conv_b"], wts["w_down"], tm=512, ck=512)


def kernel(x_prompt, x_sample, c_prompt, c_sample, norm1_w, w_in, rwkv_mu, rwkv_w0, rwkv_w2, rwkv_a0, rwkv_a2, rwkv_g2, rwkv_k_k, rwkv_k_a, rwkv_r_k, rwkv_lnx_w, rwkv_lnx_b, ssm_conv_w, ssm_conv_b, ssm_dt_bias, ssm_a_log, ssm_d, ssm_norm_w, w_out, norm2_w, ffn_w_up, ffn_conv_w, ffn_conv_b, ffn_w_down, w_ada, b_ada, final_norm_w, w_ada_final, b_ada_final):
    assert w_in.shape[0] == 1, "single-layer trunk"
    d = x_prompt.shape[-1]
    width = rwkv_w0.shape[-1]
    heads = width // HEAD
    lora = rwkv_w2.shape[2]
    n_rw = rwkv_mu.shape[-1]
    n_state = (ssm_conv_w.shape[-1] - width) // 4
    assert 2 * lora == LANES and rwkv_g2.shape[1] == LANES and 2 * heads <= LANES

    n_ssm = w_in.shape[2] - n_rw
    n_ssm_pad = -(-n_ssm // LANES) * LANES
    w_cat = _pad_lanes(w_in[0].astype(BF16), n_rw + n_ssm_pad)
    zl = jnp.zeros((lora, width), F32)
    eye_h = jnp.repeat(jnp.eye(heads, dtype=F32), HEAD, axis=1)
    seg = jnp.arange(MXU_DIM) // HEAD
    rwkv = {
        "width": width, "ncol": n_rw,
        "mu": rwkv_mu[0],
        "w0": [rwkv_w0[0, i:i + 1] for i in range(2)],
        "a0": [rwkv_a0[0, i:i + 1] for i in range(2)],
        "w2": [jnp.concatenate([rwkv_w2[0, 0], zl], 0).astype(BF16),
               jnp.concatenate([zl, rwkv_w2[0, 1]], 0).astype(BF16)],
        "a2": [jnp.concatenate([rwkv_a2[0, 0], zl], 0).astype(BF16),
               jnp.concatenate([zl, rwkv_a2[0, 1]], 0).astype(BF16)],
        "g2": rwkv_g2[0].astype(BF16),
        "k_k": rwkv_k_k, "k_a": rwkv_k_a, "r_k": rwkv_r_k[0].reshape(1, width),
        "lnx_w": rwkv_lnx_w, "lnx_b": rwkv_lnx_b,
        "bd": (seg[:, None] == seg[None, :]).astype(BF16),
    }
    ssm = {
        "width": width, "n_state": n_state,
        "conv_w": ssm_conv_w[0], "conv_b": ssm_conv_b,
        "dt_bias": _pad_lanes(ssm_dt_bias[0].reshape(1, 2 * heads), LANES),
        "a_log": _pad_lanes(ssm_a_log[0].reshape(1, 2 * heads), LANES),
        "d_skip": jnp.repeat(ssm_d[0], HEAD).reshape(1, width),
        "norm_w": ssm_norm_w,
        "expand": [_pad_rows(jnp.concatenate([eye_h * (1 - i), eye_h * i], 0), LANES).astype(BF16)
                   for i in range(2)],
    }
    wts = {
        "norm1_w": norm1_w[0], "norm2_w": norm2_w[0], "final_norm_w": final_norm_w,
        "w_in": w_cat, "w_out": w_out[0].astype(BF16),
        "w_up": ffn_w_up[0].astype(BF16), "w_down": ffn_w_down[0].astype(BF16),
        "ffn_conv_w": ffn_conv_w[0], "ffn_conv_b": ffn_conv_b[0],
        "rwkv": rwkv, "ssm": ssm,
    }

    nb_p, nb_s = c_prompt.shape[0], c_sample.shape[0]
    rows = -(-(nb_p + nb_s) // 8) * 8
    c_all = _pad_rows(jnp.concatenate([c_prompt, c_sample], 0), rows)
    mod_all = _ada(c_all, w_ada[0], b_ada[0]).reshape(rows, 6, d)
    modf_all = _ada(c_all, w_ada_final, b_ada_final).reshape(rows, 2, d)

    outs = []
    for x, lo, hi in ((x_prompt, 0, nb_p), (x_sample, nb_p, nb_p + nb_s)):
        outs.append(_encoder(x, mod_all[lo:hi], modf_all[lo:hi], wts))
    return tuple(outs)
```
